```python
import jax
import jax.numpy as jnp
from jax import lax
import numpy as np

D_MODEL = 2048
BATCH = 2
SEQ = 16384
DEPTH = 2

HEAD_DIM = 64
CHUNK = 128
BRANCH_DIM = D_MODEL // 2
N_BRANCH = 3
M_INNER = BRANCH_DIM
M_HEADDIM = HEAD_DIM
M_HEADS = M_INNER // M_HEADDIM
M_GROUPS = 4
M_HPG = M_HEADS // M_GROUPS
M_STATE = 128
M_CONV = 4
M_CONV_DIM = M_INNER + 2 * M_GROUPS * M_STATE
M_COLS = M_INNER + M_CONV_DIM + M_HEADS
M_NORM_EPS = 1e-5
RW_DIM = BRANCH_DIM
RW_HEADDIM = HEAD_DIM
RW_HEADS = RW_DIM // RW_HEADDIM
LORA_W = 96
LORA_A = 96
LORA_G = 256
RW_COLS = 3 * RW_DIM + LORA_W + LORA_A + LORA_G
RW_LN_EPS = 64e-5
RET_HEADS = 8
RET_DK = HEAD_DIM
RET_DV = BRANCH_DIM // RET_HEADS
RET_COLS = 2 * RET_HEADS * RET_DK + 2 * BRANCH_DIM
RET_EPS = 1e-6
ROPE_BASE = 10000.0
IN_COLS = M_COLS + RW_COLS + RET_COLS
FFN_DIM = 5504
FFN_CONV = 3
NORM_EPS = 1e-6

kernel_name = 'hybrid_ssd_rwkv7_retention_block'

F32 = jnp.float32


def split_cols(t, sizes):
    return jnp.split(t, np.cumsum(sizes)[:-1].tolist(), axis=-1)


def rms_norm(t, g, eps=NORM_EPS):
    tf = t.astype(F32)
    tf = tf * lax.rsqrt(jnp.mean(tf * tf, axis=-1, keepdims=True) + eps)
    return (tf * g.astype(F32)).astype(t.dtype)


def causal_dwconv(t, w, b):
    k, ch = w.shape
    y = lax.conv_general_dilated(t, w[:, None, :].astype(t.dtype), window_strides=(1,),
                                 padding=[(k - 1, 0)], dimension_numbers=('NWC', 'WIO', 'NWC'),
                                 feature_group_count=ch)
    return y + b.astype(t.dtype)


def token_shift(t):
    return jnp.pad(t, ((0, 0), (1, 0), (0, 0)))[:, :-1]


def to_chunks(t):
    b, s = t.shape[:2]
    return jnp.moveaxis(t.reshape(b, s // CHUNK, CHUNK, *t.shape[2:]), 1, 0)


def from_chunks(t):
    t = jnp.moveaxis(t, 0, 1)
    return t.reshape(t.shape[0], -1, *t.shape[3:])


def rotary(t, positions):
    half = t.shape[-1] // 2
    inv_freq = ROPE_BASE ** (-jnp.arange(half, dtype=F32) / half)
    ang = positions.astype(F32)[..., None] * inv_freq
    cos = jnp.cos(ang)[:, :, None, :]
    sin = jnp.sin(ang)[:, :, None, :]
    t1, t2 = t[..., :half], t[..., half:]
    return jnp.concatenate([t1 * cos - t2 * sin, t1 * sin + t2 * cos], axis=-1)


def mamba2_branch(proj, conv_w, conv_b, head_p, norm_g):
    bsz, s, _ = proj.shape
    z, xbc, dt_raw = split_cols(proj, [M_INNER, M_CONV_DIM, M_HEADS])
    xbc = jax.nn.silu(causal_dwconv(xbc, conv_w, conv_b))
    xs, bm, cm = split_cols(xbc, [M_INNER, M_GROUPS * M_STATE, M_GROUPS * M_STATE])
    dt_bias, a_log, d_skip = head_p[0], head_p[1], head_p[2]
    dt = jax.nn.softplus(dt_raw.astype(F32) + dt_bias.astype(F32))
    a_dt = (dt * -jnp.exp(a_log.astype(F32))).reshape(bsz, s, M_GROUPS, M_HPG)
    xs = xs.astype(F32).reshape(bsz, s, M_GROUPS, M_HPG, M_HEADDIM)
    xdt = xs * dt.reshape(bsz, s, M_GROUPS, M_HPG)[..., None]
    bm = bm.astype(F32).reshape(bsz, s, M_GROUPS, M_STATE)
    cm = cm.astype(F32).reshape(bsz, s, M_GROUPS, M_STATE)
    causal = jnp.tril(jnp.ones((CHUNK, CHUNK), dtype=bool))[None, :, :, None, None]

    def step(state, inp):
        xc, ac, bc, cc = inp
        cum = jnp.cumsum(ac, axis=1)
        seg = cum[:, :, None] - cum[:, None, :]
        decay = jnp.exp(jnp.where(causal, seg, -jnp.inf))
        cb = jnp.einsum('bign,bjgn->bijg', cc, bc)
        y = jnp.einsum('bijg,bijgr,bjgrp->bigrp', cb, decay, xc)
        y = y + jnp.einsum('bign,bgrpn->bigrp', cc, state) * jnp.exp(cum)[..., None]
        to_end = jnp.exp(cum[:, -1:] - cum)
        state = (state * jnp.exp(cum[:, -1])[..., None, None]
                 + jnp.einsum('bjgn,bjgr,bjgrp->bgrpn', bc, to_end, xc))
        return state, y

    state0 = jnp.zeros((bsz, M_GROUPS, M_HPG, M_HEADDIM, M_STATE), F32)
    _, ys = lax.scan(step, state0, (to_chunks(xdt), to_chunks(a_dt), to_chunks(bm), to_chunks(cm)))
    y = from_chunks(ys) + xs * d_skip.astype(F32).reshape(M_GROUPS, M_HPG)[:, :, None]
    y = y.reshape(bsz, s, M_INNER) * jax.nn.silu(z.astype(F32))
    y = rms_norm(y.reshape(bsz, s, M_GROUPS, M_INNER // M_GROUPS),
                 norm_g.reshape(M_GROUPS, M_INNER // M_GROUPS), M_NORM_EPS)
    return y.reshape(bsz, s, M_INNER).astype(proj.dtype)


def rwkv7_branch(proj, mu, w2, a2, g2, vec):
    bsz, s, _ = proj.shape
    proj = proj + (token_shift(proj) - proj) * mu
    r, k, v, wd, ad, gd = split_cols(proj, [RW_DIM, RW_DIM, RW_DIM, LORA_W, LORA_A, LORA_G])
    w0, a0, k_k, k_a, r_k, ln_w, ln_b = [vec[i] for i in range(7)]
    log_w = -jax.nn.softplus(-(w0 + jnp.tanh(wd) @ w2).astype(F32)) - 0.5
    decay = jnp.exp(-jnp.exp(log_w))
    a = jax.nn.sigmoid((a0 + ad @ a2).astype(F32))
    g = (jax.nn.sigmoid(gd) @ g2).astype(F32)

    def heads(t):
        return t.astype(F32).reshape(bsz, s, RW_HEADS, RW_HEADDIM)

    kk = heads(k * k_k)
    kk = kk * lax.rsqrt(jnp.maximum(jnp.sum(kk * kk, axis=-1, keepdims=True), 1e-24))
    k_h = heads(k.astype(F32) * (1.0 + (a - 1.0) * k_a.astype(F32)))
    r_h, v_h, a_h, w_h = heads(r), heads(v), heads(a), heads(decay)

    def step(state, inp):
        r_t, w_t, k_t, v_t, kk_t, b_t = inp
        sa = jnp.einsum('bhvk,bhk->bhv', state, kk_t)
        state = (state * w_t[:, :, None, :] - sa[..., None] * b_t[:, :, None, :]
                 + v_t[..., None] * k_t[:, :, None, :])
        return state, jnp.einsum('bhvk,bhk->bhv', state, r_t)

    seq_major = tuple(jnp.moveaxis(t, 1, 0) for t in (r_h, w_h, k_h, v_h, kk, kk * a_h))
    state0 = jnp.zeros((bsz, RW_HEADS, RW_HEADDIM, RW_HEADDIM), F32)
    _, ys = lax.scan(step, state0, seq_major)
    y = jnp.moveaxis(ys, 0, 1)
    mean = jnp.mean(y, axis=-1, keepdims=True)
    var = jnp.mean(jnp.square(y - mean), axis=-1, keepdims=True)
    y = ((y - mean) * lax.rsqrt(var + RW_LN_EPS) * ln_w.astype(F32).reshape(RW_HEADS, RW_HEADDIM)
         + ln_b.astype(F32).reshape(RW_HEADS, RW_HEADDIM))
    bonus = jnp.sum(r_h * k_h * r_k.astype(F32).reshape(RW_HEADS, RW_HEADDIM), axis=-1, keepdims=True) * v_h
    y = (y + bonus).reshape(bsz, s, RW_DIM) * g
    return y.astype(proj.dtype)


def retention_branch(proj, positions):
    bsz, s, _ = proj.shape
    q, k, v, g = split_cols(proj, [RET_HEADS * RET_DK, RET_HEADS * RET_DK, BRANCH_DIM, BRANCH_DIM])
    q = rotary(q.astype(F32).reshape(bsz, s, RET_HEADS, RET_DK), positions)
    k = rotary(k.astype(F32).reshape(bsz, s, RET_HEADS, RET_DK), positions) * (RET_DK ** -0.5)
    v = v.astype(F32).reshape(bsz, s, RET_HEADS, RET_DV)
    log_gamma = jnp.log1p(-jnp.exp2(-5.0 - jnp.arange(RET_HEADS, dtype=F32)))
    idx = jnp.arange(CHUNK, dtype=F32)
    rel = (idx[:, None] - idx[None, :])[None]
    dmask = jnp.exp(jnp.where(rel >= 0, rel * log_gamma[:, None, None], -jnp.inf))
    q_dec = jnp.exp((idx[:, None] + 1.0) * log_gamma)
    k_dec = jnp.exp((CHUNK - 1.0 - idx)[:, None] * log_gamma)
    c_dec = jnp.exp(CHUNK * log_gamma)

    def step(state, inp):
        qc, kc, vc = inp
        scores = jnp.einsum('bihd,bjhd->bhij', qc, kc) * dmask
        y = jnp.einsum('bhij,bjhe->bihe', scores, vc)
        y = y + jnp.einsum('bihd,bhde->bihe', qc * q_dec[:, :, None], state)
        state = state * c_dec[:, None, None] + jnp.einsum('bjhd,bjhe->bhde', kc * k_dec[:, :, None], vc)
        return state, y

    state0 = jnp.zeros((bsz, RET_HEADS, RET_DK, RET_DV), F32)
    _, ys = lax.scan(step, state0, (to_chunks(q), to_chunks(k), to_chunks(v)))
    y = from_chunks(ys)
    y = y * lax.rsqrt(jnp.mean(y * y, axis=-1, keepdims=True) + RET_EPS)
    y = jax.nn.silu(g.astype(F32)) * y.reshape(bsz, s, BRANCH_DIM)
    return y.astype(proj.dtype)


def hybrid_mixer(h, positions, w_in, m_conv_w, m_conv_b, m_head, m_norm_g, rwkv_mu, rwkv_w2,
                 rwkv_a2, rwkv_g2, rwkv_vec, w_gate, w_branch, w_out):
    proj = h @ w_in
    m_proj, rw_proj, ret_proj = split_cols(proj, [M_COLS, RW_COLS, RET_COLS])
    y_ssd = mamba2_branch(m_proj, m_conv_w, m_conv_b, m_head, m_norm_g)
    y_rwkv = rwkv7_branch(rw_proj, rwkv_mu, rwkv_w2, rwkv_a2, rwkv_g2, rwkv_vec)
    y_ret = retention_branch(ret_proj, positions)
    merged = jax.nn.sigmoid(h @ w_gate[0]) * (y_ssd @ w_branch[0])
    merged = merged + jax.nn.sigmoid(h @ w_gate[1]) * (y_rwkv @ w_branch[1])
    merged = merged + jax.nn.sigmoid(h @ w_gate[2]) * (y_ret @ w_branch[2])
    return merged @ w_out


def conv_geglu(h, w_up, conv_w, conv_b, w_down):
    u = causal_dwconv(h @ w_up, conv_w, conv_b)
    gate, up = jnp.split(u, 2, axis=-1)
    return (jax.nn.gelu(gate, approximate=True) * up) @ w_down


def setup_inputs(seed: int = 0) -> dict:
    key = jax.random.key(seed)
    ks = jax.random.split(key, 32)

    def nrm(k, shape, scale):
        return jax.random.normal(k, shape, F32) * scale

    def unif(k, shape, lo, hi):
        return jax.random.uniform(k, shape, F32, lo, hi)

    x = nrm(ks[0], (BATCH, SEQ, D_MODEL), 1.0)
    c = nrm(ks[1], (BATCH, D_MODEL), 1.0)
    positions = jnp.broadcast_to(jnp.arange(SEQ, dtype=jnp.int32), (BATCH, SEQ))
    w_ada = nrm(ks[2], (DEPTH, D_MODEL, 6 * D_MODEL), 0.5 * D_MODEL ** -0.5)
    b_ada = nrm(ks[3], (DEPTH, 6 * D_MODEL), 0.02)
    norm_g = 1.0 + nrm(ks[4], (DEPTH, 4, D_MODEL), 0.02)
    w_in = nrm(ks[5], (DEPTH, D_MODEL, IN_COLS), D_MODEL ** -0.5)
    m_conv_w = nrm(ks[6], (DEPTH, M_CONV, M_CONV_DIM), 0.5)
    m_conv_b = nrm(ks[7], (DEPTH, M_CONV_DIM), 0.02)
    dt0 = jnp.exp(unif(ks[8], (DEPTH, M_HEADS), float(np.log(1e-3)), float(np.log(1e-1))))
    dt_bias = dt0 + jnp.log(-jnp.expm1(-dt0))
    a_log = jnp.log(unif(ks[9], (DEPTH, M_HEADS), 1.0, 16.0))
    d_skip = 1.0 + nrm(ks[10], (DEPTH, M_HEADS), 0.02)
    m_head = jnp.stack([dt_bias, a_log, d_skip], axis=1)
    m_norm_g = 1.0 + nrm(ks[11], (DEPTH, M_INNER), 0.02)
    rwkv_mu = unif(ks[12], (DEPTH, RW_COLS), 0.0, 1.0)
    rwkv_w2 = nrm(ks[13], (DEPTH, LORA_W, RW_DIM), 0.1 * LORA_W ** -0.5)
    rwkv_a2 = nrm(ks[14], (DEPTH, LORA_A, RW_DIM), 0.1 * LORA_A ** -0.5)
    rwkv_g2 = nrm(ks[15], (DEPTH, LORA_G, RW_DIM), LORA_G ** -0.5)
    rw_w0 = unif(ks[16], (DEPTH, RW_DIM), -6.5, -1.5)
    rw_a0 = nrm(ks[17], (DEPTH, RW_DIM), 0.1)
    rw_kk = 0.85 + nrm(ks[18], (DEPTH, RW_DIM), 0.05)
    rw_ka = 1.0 + nrm(ks[19], (DEPTH, RW_DIM), 0.05)
    rw_rk = nrm(ks[20], (DEPTH, RW_DIM), 0.1)
    rw_lnw = 1.0 + nrm(ks[21], (DEPTH, RW_DIM), 0.02)
    rw_lnb = nrm(ks[22], (DEPTH, RW_DIM), 0.02)
    rwkv_vec = jnp.stack([rw_w0, rw_a0, rw_kk, rw_ka, rw_rk, rw_lnw, rw_lnb], axis=1)
    w_gate = nrm(ks[23], (DEPTH, N_BRANCH, D_MODEL, D_MODEL), D_MODEL ** -0.5)
    w_branch = nrm(ks[24], (DEPTH, N_BRANCH, BRANCH_DIM, D_MODEL), BRANCH_DIM ** -0.5)
    w_out = nrm(ks[25], (DEPTH, D_MODEL, D_MODEL), D_MODEL ** -0.5)
    w_up = nrm(ks[26], (DEPTH, D_MODEL, 2 * FFN_DIM), D_MODEL ** -0.5)
    f_conv_w = nrm(ks[27], (DEPTH, FFN_CONV, 2 * FFN_DIM), FFN_CONV ** -0.5)
    f_conv_b = nrm(ks[28], (DEPTH, 2 * FFN_DIM), 0.02)
    w_down = nrm(ks[29], (DEPTH, FFN_DIM, D_MODEL), FFN_DIM ** -0.5)
    return {'x': x, 'c': c, 'positions': positions, 'w_ada': w_ada, 'b_ada': b_ada,
            'norm_g': norm_g, 'w_in': w_in, 'm_conv_w': m_conv_w, 'm_conv_b': m_conv_b,
            'm_head': m_head, 'm_norm_g': m_norm_g, 'rwkv_mu': rwkv_mu, 'rwkv_w2': rwkv_w2,
            'rwkv_a2': rwkv_a2, 'rwkv_g2': rwkv_g2, 'rwkv_vec': rwkv_vec, 'w_gate': w_gate,
            'w_branch': w_branch, 'w_out': w_out, 'w_up': w_up, 'f_conv_w': f_conv_w,
            'f_conv_b': f_conv_b, 'w_down': w_down}


def reference(x, c, positions, w_ada, b_ada, norm_g, w_in, m_conv_w, m_conv_b, m_head, m_norm_g,
              rwkv_mu, rwkv_w2, rwkv_a2, rwkv_g2, rwkv_vec, w_gate, w_branch, w_out,
              w_up, f_conv_w, f_conv_b, w_down):
    for l in range(DEPTH):
        mod = (jax.nn.silu(c) @ w_ada[l] + b_ada[l])[:, None, :]
        sh_m, sc_m, gt_m, sh_f, sc_f, gt_f = jnp.split(mod, 6, axis=-1)
        h = rms_norm(x, norm_g[l, 0]) * (1.0 + sc_m) + sh_m
        y = hybrid_mixer(h, positions, w_in[l], m_conv_w[l], m_conv_b[l], m_head[l], m_norm_g[l],
                         rwkv_mu[l], rwkv_w2[l], rwkv_a2[l], rwkv_g2[l], rwkv_vec[l],
                         w_gate[l], w_branch[l], w_out[l])
        x = x + gt_m * rms_norm(y, norm_g[l, 1])
        h = rms_norm(x, norm_g[l, 2]) * (1.0 + sc_f) + sh_f
        y = conv_geglu(h, w_up[l], f_conv_w[l], f_conv_b[l], w_down[l])
        x = x + gt_f * rms_norm(y, norm_g[l, 3])
    return x
```

```python
import functools

import numpy as np
import jax
import jax.numpy as jnp
from jax import lax
from jax.experimental import pallas as pl
from jax.experimental.pallas import tpu as pltpu

F32 = jnp.float32
BF16 = jnp.bfloat16
HIGHEST = lax.Precision.HIGHEST

D_MODEL = 2048
HEAD_DIM = 64
BRANCH_DIM = D_MODEL // 2
SSD_CHUNK = 128
M_HEADS = 16
M_GROUPS = 4
M_STATE = 128
M_CONV = 4
M_NORM_EPS = 1e-5
RW_CHUNK = 64
RW_LN_EPS = 64e-5
LORA_W, LORA_A, LORA_G = 96, 96, 256
RET_HEADS = 8
RET_DK = 64
RET_EPS = 1e-6
ROPE_BASE = 10000.0
FFN_DIM = 5504
FFN_PAD = 5632
FFN_TILE = 512
NORM_EPS = 1e-6
LANES = 128
SUBLANES = 8
TAIL = 512
VMEM_LIMIT = 52 * 1024 * 1024

COL_Z, COL_XS, COL_BC = 0, 1024, 2048
COL_RKV = 3072
COL_RQ, COL_RK, COL_RV, COL_RG = 6144, 6656, 7168, 8192
COL_TAIL = 9216
PROJ_COLS = COL_TAIL + TAIL
TAIL_WD, TAIL_AD, TAIL_GD, TAIL_DT = 0, 96, 192, 448


def _cparams(n_axes):
    return pltpu.CompilerParams(dimension_semantics=("arbitrary",) * n_axes,
                                vmem_limit_bytes=VMEM_LIMIT)


def _bdot(a, b):
    return jnp.dot(a.astype(BF16), b.astype(BF16), preferred_element_type=F32)


def _bdot_nt(a, b):
    return lax.dot_general(a.astype(BF16), b.astype(BF16), (((1,), (1,)), ((), ())),
                           preferred_element_type=F32)


def _bdot_tn(a, b):
    return lax.dot_general(a.astype(BF16), b.astype(BF16), (((0,), (0,)), ((), ())),
                           preferred_element_type=F32)


def _hdot(a, b):
    return jnp.dot(a, b, preferred_element_type=F32, precision=HIGHEST)


def _hdot_nt(a, b):
    return lax.dot_general(a, b, (((1,), (1,)), ((), ())), preferred_element_type=F32,
                           precision=HIGHEST)


def _silu(t):
    return t * jax.nn.sigmoid(t)


def _softplus(t):
    return jnp.maximum(t, 0.0) + jnp.log1p(jnp.exp(-jnp.abs(t)))


def _rms(t, eps):
    return t * lax.rsqrt(jnp.mean(t * t, axis=-1, keepdims=True) + eps)


def _shift_rows(cur, prev, k):
    rolled = pltpu.roll(cur, k, 0)
    prev8 = pltpu.roll(prev[prev.shape[0] - SUBLANES:], k, 0)
    row = lax.broadcasted_iota(jnp.int32, (SUBLANES, cur.shape[1]), 0)
    top = jnp.where(row < k, prev8, rolled[:SUBLANES])
    return jnp.concatenate([top, rolled[SUBLANES:]], axis=0)


def _ada_kernel(c_ref, w_ref, b_ref, o_ref):
    o_ref[...] = _bdot(_silu(c_ref[...]), w_ref[...]) + b_ref[...]


def ada_mod(c_pad, w, b):
    rows, d = c_pad.shape
    cols = w.shape[1]
    tn = 1024
    return pl.pallas_call(
        _ada_kernel,
        grid=(cols // tn,),
        in_specs=[pl.BlockSpec((rows, d), lambda j: (0, 0)),
                  pl.BlockSpec((d, tn), lambda j: (0, j)),
                  pl.BlockSpec((1, tn), lambda j: (0, j))],
        out_specs=pl.BlockSpec((rows, tn), lambda j: (0, j)),
        out_shape=jax.ShapeDtypeStruct((rows, cols), F32),
        compiler_params=_cparams(1),
        name="ada_mod",
    )(c_pad, w, b.reshape(1, cols))


def _modulated_norm(x, g, sc, sh):
    return _rms(x, NORM_EPS) * g * (1.0 + sc) + sh


def _norm_proj_kernel(x_ref, g_ref, sc_ref, sh_ref, w_ref, o_ref, h_ref):
    @pl.when(pl.program_id(1) == 0)
    def _():
        h_ref[...] = _modulated_norm(x_ref[...], g_ref[...], sc_ref[0], sh_ref[0]).astype(BF16)

    o_ref[...] = jnp.dot(h_ref[...], w_ref[...], preferred_element_type=F32)


def norm_proj(x, g, sc, sh, w, seq):
    n, d = x.shape
    cols = w.shape[1]
    tm = min(512, seq)
    tn = 512
    per_batch = seq // tm
    return pl.pallas_call(
        _norm_proj_kernel,
        grid=(n // tm, cols // tn),
        in_specs=[pl.BlockSpec((tm, d), lambda i, j: (i, 0)),
                  pl.BlockSpec((1, d), lambda i, j: (0, 0)),
                  pl.BlockSpec((1, 1, d), lambda i, j: (i // per_batch, 0, 0)),
                  pl.BlockSpec((1, 1, d), lambda i, j: (i // per_batch, 0, 0)),
                  pl.BlockSpec((d, tn), lambda i, j: (0, j))],
        out_specs=[pl.BlockSpec((tm, tn), lambda i, j: (i, j)),
                   pl.BlockSpec((tm, d), lambda i, j: (i, 0))],
        out_shape=[jax.ShapeDtypeStruct((n, cols), F32),
                   jax.ShapeDtypeStruct((n, d), BF16)],
        compiler_params=_cparams(2),
        name="norm_proj",
    )(x, g, sc, sh, w)


def _ssd_kernel(z_ref, xs_ref, bc_ref, tail_ref, cw_ref, cb_ref, dtb_ref, alog_ref, dskip_ref,
                ng_ref, expand_ref, o_ref, prev_s, state_s):
    L = SSD_CHUNK

    @pl.when(pl.program_id(1) == 0)
    def _():
        prev_s[...] = jnp.zeros_like(prev_s)
        state_s[...] = jnp.zeros_like(state_s)

    xbc = jnp.concatenate([xs_ref[...], bc_ref[...]], axis=1)
    prev = prev_s[...]
    prev_s[...] = xbc
    cw = cw_ref[...]
    acc = cw[M_CONV - 1:M_CONV] * xbc + cb_ref[...]
    for k in range(1, M_CONV):
        acc = acc + cw[M_CONV - 1 - k:M_CONV - k] * _shift_rows(xbc, prev, k)
    xc = _silu(acc)
    xs = xc[:, :BRANCH_DIM]
    bm = xc[:, BRANCH_DIM:BRANCH_DIM + M_GROUPS * M_STATE]
    cm = xc[:, BRANCH_DIM + M_GROUPS * M_STATE:]

    dt = _softplus(tail_ref[:, TAIL_DT:TAIL_DT + M_HEADS] + dtb_ref[...])
    a_dt = dt * -jnp.exp(alog_ref[...])
    ri = lax.broadcasted_iota(jnp.int32, (L, L), 0)
    ci = lax.broadcasted_iota(jnp.int32, (L, L), 1)
    causal = ri >= ci
    tri = causal.astype(F32)
    cum = _hdot(tri, a_dt)
    eye_h = (lax.broadcasted_iota(jnp.int32, (M_HEADS, M_HEADS), 0)
             == lax.broadcasted_iota(jnp.int32, (M_HEADS, M_HEADS), 1)).astype(F32)
    cum_t = _hdot_nt(eye_h, cum)

    expand = expand_ref[...]
    dt_full = _hdot(dt, expand)
    cum_full = _hdot(cum, expand)
    cum_last = cum_full[L - 1:L, :]
    xdt = xs * dt_full
    ecum = jnp.exp(cum_full)
    to_end = jnp.exp(cum_last - cum_full)
    elast = jnp.exp(cum_last)

    lane = lax.broadcasted_iota(jnp.int32, (L, LANES), 1)
    first = lane < HEAD_DIM
    ys = []
    for p in range(M_HEADS // 2):
        g = p // 2
        lo = p * LANES
        bg = bm[:, g * M_STATE:(g + 1) * M_STATE]
        cg = cm[:, g * M_STATE:(g + 1) * M_STATE]
        cb = _bdot_nt(cg, bg)
        ms = []
        for h in (2 * p, 2 * p + 1):
            seg = cum[:, h:h + 1] - cum_t[h:h + 1, :]
            ms.append(cb * jnp.exp(jnp.where(causal, seg, -1e30)))
        xp = xdt[:, lo:lo + LANES]
        x2 = jnp.concatenate([jnp.where(first, xp, 0.0), jnp.where(first, 0.0, xp)], axis=0)
        y = _bdot(jnp.concatenate(ms, axis=1), x2)
        st = state_s[p]
        y = y + _bdot(cg, st) * ecum[:, lo:lo + LANES]
        state_s[p] = st * elast[:, lo:lo + LANES] + _bdot_tn(bg, xp * to_end[:, lo:lo + LANES])
        ys.append(y)
    y = jnp.concatenate(ys, axis=1) + xs * dskip_ref[...]
    y = y * _silu(z_ref[...])
    gw = BRANCH_DIM // M_GROUPS
    y = jnp.concatenate([_rms(y[:, g * gw:(g + 1) * gw], M_NORM_EPS) for g in range(M_GROUPS)],
                        axis=1)
    o_ref[...] = (y * ng_ref[...]).astype(BF16)


def ssd_branch(proj, conv_w, conv_b, head_p, norm_g, batch, seq):
    n = proj.shape[0]
    L = SSD_CHUNK
    nc = seq // L
    expand = jnp.repeat(jnp.eye(M_HEADS, dtype=F32), HEAD_DIM, axis=1)
    dskip = jnp.repeat(head_p[2], HEAD_DIM).reshape(1, BRANCH_DIM)

    def rows(width, idx):
        return pl.BlockSpec((L, width), lambda b, c: (b * nc + c, idx))

    def const(shape):
        return pl.BlockSpec(shape, lambda b, c: (0,) * len(shape))

    return pl.pallas_call(
        _ssd_kernel,
        grid=(batch, nc),
        in_specs=[rows(1024, COL_Z // 1024), rows(1024, COL_XS // 1024), rows(1024, COL_BC // 1024),
                  rows(TAIL, COL_TAIL // TAIL),
                  const((M_CONV, 2048)), const((1, 2048)), const((1, M_HEADS)), const((1, M_HEADS)),
                  const((1, BRANCH_DIM)), const((1, BRANCH_DIM)), const((M_HEADS, BRANCH_DIM))],
        out_specs=rows(BRANCH_DIM, 0),
        out_shape=jax.ShapeDtypeStruct((n, BRANCH_DIM), BF16),
        scratch_shapes=[pltpu.VMEM((L, 2048), F32),
                        pltpu.VMEM((M_HEADS // 2, M_STATE, LANES), F32)],
        compiler_params=_cparams(2),
        name="ssd_branch",
    )(proj, proj, proj, proj, conv_w, conv_b.reshape(1, -1), head_p[0].reshape(1, -1),
      head_p[1].reshape(1, -1), dskip, norm_g.reshape(1, -1), expand)


def _rwkv_kernel(rkv_ref, tail_ref, mu_rkv_ref, mu_tail_ref, w2_ref, a2_ref, g2_ref, vec_ref,
                 o_ref, prev_rkv_s, prev_tail_s, state_s):
    C = RW_CHUNK
    R = 2 * C

    @pl.when(pl.program_id(1) == 0)
    def _():
        prev_rkv_s[...] = jnp.zeros_like(prev_rkv_s)
        prev_tail_s[...] = jnp.zeros_like(prev_tail_s)
        state_s[...] = jnp.zeros_like(state_s)

    rkv_raw = rkv_ref[...]
    tail_raw = tail_ref[...]
    rkv = rkv_raw + (_shift_rows(rkv_raw, prev_rkv_s[...], 1) - rkv_raw) * mu_rkv_ref[...]
    tail = tail_raw + (_shift_rows(tail_raw, prev_tail_s[...], 1) - tail_raw) * mu_tail_ref[...]
    prev_rkv_s[...] = rkv_raw[C - SUBLANES:]
    prev_tail_s[...] = tail_raw[C - SUBLANES:]
    r = rkv[:, :BRANCH_DIM]
    k = rkv[:, BRANCH_DIM:2 * BRANCH_DIM]
    v = rkv[:, 2 * BRANCH_DIM:]

    vec = vec_ref[...]
    w0, a0, k_k, k_a, r_k, ln_w, ln_b = [vec[i:i + 1] for i in range(7)]
    log_w = -_softplus(-(w0 + _bdot(jnp.tanh(tail), w2_ref[...]))) - 0.5
    neg_log_decay = jnp.exp(log_w)
    a = jax.nn.sigmoid(a0 + _bdot(tail, a2_ref[...]))
    g = _bdot(jax.nn.sigmoid(tail), g2_ref[...])

    lane = lax.broadcasted_iota(jnp.int32, (LANES, LANES), 1)
    sub = lax.broadcasted_iota(jnp.int32, (LANES, LANES), 0)
    same_head = (lane >> 6) == (sub >> 6)
    head_ones = same_head.astype(F32)
    strict = same_head & (sub > lane)
    incl = same_head & (sub >= lane)
    eye = (sub == lane).astype(F32)
    first = lax.broadcasted_iota(jnp.int32, (C, LANES), 1) < HEAD_DIM

    ct = lax.broadcasted_iota(jnp.int32, (C, C), 0) >= lax.broadcasted_iota(jnp.int32, (C, C), 1)
    cl = _hdot(ct.astype(F32), -neg_log_decay)
    cl_last = cl[C - 1:C]
    w_incl = jnp.exp(cl)
    w_inv = jnp.exp(-cl)
    w_excl = jnp.exp(cl + neg_log_decay)
    w_end = jnp.exp(cl_last - cl)
    w_chunk = jnp.exp(cl_last)

    kk = k * k_k
    k2 = k * (1.0 + (a - 1.0) * k_a)

    def stack(t):
        return jnp.concatenate([jnp.where(first, t, 0.0), jnp.where(first, 0.0, t)], axis=0)

    def dup(t):
        return jnp.concatenate([t, t], axis=0)

    outs = []
    for p in range(BRANCH_DIM // LANES):
        sl = slice(p * LANES, (p + 1) * LANES)
        kkp = kk[:, sl]
        kkp = kkp * lax.rsqrt(jnp.maximum(_hdot(kkp * kkp, head_ones), 1e-24))
        bp = kkp * a[:, sl]
        k2p, rp, vp = k2[:, sl], r[:, sl], v[:, sl]
        lk = stack(kkp * w_excl[:, sl])
        lr = stack(rp * w_incl[:, sl])
        rb = dup(bp * w_inv[:, sl])
        rk = dup(k2p * w_inv[:, sl])
        a_b = jnp.where(strict, _hdot_nt(lk, rb), 0.0)
        a_k = jnp.where(strict, _hdot_nt(lk, rk), 0.0)
        l_b = jnp.where(incl, _hdot_nt(lr, rb), 0.0)
        l_k = jnp.where(incl, _hdot_nt(lr, rk), 0.0)
        minv = eye - a_b
        pw = a_b
        for _ in range(int(np.log2(C)) - 1):
            pw = _hdot(pw, pw)
            minv = minv + _hdot(minv, pw)
        vs = stack(vp)
        st = state_s[p]
        u = _hdot(minv, -_bdot_nt(lk, st) - _bdot(a_k, vs))
        ysk = _bdot_nt(lr, st) + _bdot(l_b, u) + _bdot(l_k, vs)
        y = ysk[:C] + ysk[C:]
        state_s[p] = (st * w_chunk[:, sl] + _bdot_tn(u, stack(bp * w_end[:, sl]))
                      + _bdot_tn(vs, stack(k2p * w_end[:, sl])))
        mean = _hdot(y, head_ones) * (1.0 / HEAD_DIM)
        dev = y - mean
        var = _hdot(dev * dev, head_ones) * (1.0 / HEAD_DIM)
        yn = dev * lax.rsqrt(var + RW_LN_EPS) * ln_w[:, sl] + ln_b[:, sl]
        bonus = _hdot(rp * k2p * r_k[:, sl], head_ones) * vp
        outs.append((yn + bonus) * g[:, sl])
    o_ref[...] = jnp.concatenate(outs, axis=1).astype(BF16)


def rwkv_branch(proj, mu_rkv, mu_tail, w2p, a2p, g2p, vec, batch, seq):
    n = proj.shape[0]
    C = RW_CHUNK
    nc = seq // C

    def rows(width, idx):
        return pl.BlockSpec((C, width), lambda b, c: (b * nc + c, idx))

    def const(shape):
        return pl.BlockSpec(shape, lambda b, c: (0,) * len(shape))

    return pl.pallas_call(
        _rwkv_kernel,
        grid=(batch, nc),
        in_specs=[rows(3 * BRANCH_DIM, COL_RKV // (3 * BRANCH_DIM)), rows(TAIL, COL_TAIL // TAIL),
                  const((1, 3 * BRANCH_DIM)), const((1, TAIL)),
                  const((TAIL, BRANCH_DIM)), const((TAIL, BRANCH_DIM)), const((TAIL, BRANCH_DIM)),
                  const((7, BRANCH_DIM))],
        out_specs=rows(BRANCH_DIM, 0),
        out_shape=jax.ShapeDtypeStruct((n, BRANCH_DIM), BF16),
        scratch_shapes=[pltpu.VMEM((SUBLANES, 3 * BRANCH_DIM), F32),
                        pltpu.VMEM((SUBLANES, TAIL), F32),
                        pltpu.VMEM((BRANCH_DIM // LANES, LANES, LANES), F32)],
        compiler_params=_cparams(2),
        name="rwkv_branch",
    )(proj, proj, mu_rkv, mu_tail, w2p, a2p, g2p, vec)


def _ret_kernel(q_ref, k_ref, v_ref, g_ref, pos_ref, invf_ref, dmask_ref, qdec_ref, kdec_ref,
                cdec_ref, o_ref, state_s):
    L = SSD_CHUNK
    half = RET_DK // 2
    width = RET_HEADS * RET_DK

    @pl.when(pl.program_id(1) == 0)
    def _():
        state_s[...] = jnp.zeros_like(state_s)

    ang = pos_ref[...].astype(F32) * invf_ref[...]
    cos = jnp.concatenate([jnp.cos(ang)] * (width // LANES), axis=1)
    sin = jnp.concatenate([jnp.sin(ang)] * (width // LANES), axis=1)
    lane_w = lax.broadcasted_iota(jnp.int32, (L, width), 1)
    low_half = (lane_w & (RET_DK - 1)) < half

    def rope(t):
        rot = jnp.where(low_half, -pltpu.roll(t, width - half, 1), pltpu.roll(t, half, 1))
        return t * cos + rot * sin

    q = rope(q_ref[...])
    k = rope(k_ref[...]) * (RET_DK ** -0.5)
    qd = q * qdec_ref[...]
    kd = k * kdec_ref[...]
    v = v_ref[...]
    g = g_ref[...]
    lane = lax.broadcasted_iota(jnp.int32, (L, LANES), 1)
    outs = []
    for h in range(RET_HEADS):
        lo = (h // 2) * LANES
        mine = (lane >> 6) == (h % 2)
        qh = jnp.where(mine, q[:, lo:lo + LANES], 0.0)
        qdh = jnp.where(mine, qd[:, lo:lo + LANES], 0.0)
        kdh = jnp.where(mine, kd[:, lo:lo + LANES], 0.0)
        vh = v[:, h * LANES:(h + 1) * LANES]
        scores = _bdot_nt(qh, k[:, lo:lo + LANES]) * dmask_ref[h]
        st = state_s[h]
        y = _bdot(scores, vh) + _bdot(qdh, st)
        state_s[h] = st * cdec_ref[h] + _bdot_tn(kdh, vh)
        outs.append(_rms(y, RET_EPS))
    o_ref[...] = (_silu(g) * jnp.concatenate(outs, axis=1)).astype(BF16)


def ret_branch(proj, pos_col, batch, seq):
    n = proj.shape[0]
    L = SSD_CHUNK
    nc = seq // L
    half = RET_DK // 2
    inv_freq = ROPE_BASE ** (-jnp.arange(half, dtype=F32) / half)
    invf = jnp.tile(inv_freq, LANES // half).reshape(1, LANES)
    log_gamma = jnp.log1p(-jnp.exp2(-5.0 - jnp.arange(RET_HEADS, dtype=F32)))
    idx = jnp.arange(L, dtype=F32)
    rel = (idx[:, None] - idx[None, :])[None]
    dmask = jnp.exp(jnp.where(rel >= 0, rel * log_gamma[:, None, None], -jnp.inf))
    qdec = jnp.repeat(jnp.exp((idx[:, None] + 1.0) * log_gamma), RET_DK, axis=1)
    kdec = jnp.repeat(jnp.exp((L - 1.0 - idx)[:, None] * log_gamma), RET_DK, axis=1)
    cdec = jnp.broadcast_to(jnp.exp(L * log_gamma)[:, None, None], (RET_HEADS, 1, LANES))

    def rows(width, idx_):
        return pl.BlockSpec((L, width), lambda b, c: (b * nc + c, idx_))

    def const(shape):
        return pl.BlockSpec(shape, lambda b, c: (0,) * len(shape))

    return pl.pallas_call(
        _ret_kernel,
        grid=(batch, nc),
        in_specs=[rows(512, COL_RQ // 512), rows(512, COL_RK // 512), rows(1024, COL_RV // 1024),
                  rows(1024, COL_RG // 1024), rows(1, 0),
                  const((1, LANES)), const((RET_HEADS, L, L)), const((L, 512)), const((L, 512)),
                  const((RET_HEADS, 1, LANES))],
        out_specs=rows(BRANCH_DIM, 0),
        out_shape=jax.ShapeDtypeStruct((n, BRANCH_DIM), BF16),
        scratch_shapes=[pltpu.VMEM((RET_HEADS, LANES, LANES), F32)],
        compiler_params=_cparams(2),
        name="ret_branch",
    )(proj, proj, proj, proj, pos_col, invf, dmask, qdec, kdec, cdec)


def _merge_kernel(h_ref, y0_ref, y1_ref, y2_ref, wg_ref, wb_ref, o_ref):
    h = h_ref[...]
    acc = None
    for b, y_ref in enumerate((y0_ref, y1_ref, y2_ref)):
        gate = jax.nn.sigmoid(jnp.dot(h, wg_ref[b], preferred_element_type=F32))
        term = gate * jnp.dot(y_ref[...], wb_ref[b], preferred_element_type=F32)
        acc = term if acc is None else acc + term
    o_ref[...] = acc.astype(BF16)


def merge_branches(h, ys, wg, wb, seq):
    n, d = h.shape
    tm = min(1024, seq)
    tn = 512
    yspec = pl.BlockSpec((tm, BRANCH_DIM), lambda i, j: (i, 0))
    return pl.pallas_call(
        _merge_kernel,
        grid=(n // tm, d // tn),
        in_specs=[pl.BlockSpec((tm, d), lambda i, j: (i, 0)), yspec, yspec, yspec,
                  pl.BlockSpec((3, d, tn), lambda i, j: (0, 0, j)),
                  pl.BlockSpec((3, BRANCH_DIM, tn), lambda i, j: (0, 0, j))],
        out_specs=pl.BlockSpec((tm, tn), lambda i, j: (i, j)),
        out_shape=jax.ShapeDtypeStruct((n, d), BF16),
        compiler_params=_cparams(2),
        name="merge_branches",
    )(h, *ys, wg, wb)


def _proj_residual_kernel(a_ref, w_ref, x_ref, g_ref, gt_ref, o_ref):
    y = jnp.dot(a_ref[...], w_ref[...], preferred_element_type=F32)
    o_ref[...] = x_ref[...] + gt_ref[0] * (_rms(y, NORM_EPS) * g_ref[...])


def proj_residual(a, w, x, g, gt, seq):
    n, d = x.shape
    tm = min(512, seq)
    per_batch = seq // tm
    return pl.pallas_call(
        _proj_residual_kernel,
        grid=(n // tm,),
        in_specs=[pl.BlockSpec((tm, d), lambda i: (i, 0)),
                  pl.BlockSpec((d, d), lambda i: (0, 0)),
                  pl.BlockSpec((tm, d), lambda i: (i, 0)),
                  pl.BlockSpec((1, d), lambda i: (0, 0)),
                  pl.BlockSpec((1, 1, d), lambda i: (i // per_batch, 0, 0))],
        out_specs=pl.BlockSpec((tm, d), lambda i: (i, 0)),
        out_shape=jax.ShapeDtypeStruct((n, d), F32),
        compiler_params=_cparams(1),
        name="proj_residual",
    )(a, w, x, g, gt)


def _gelu_tanh(t):
    return 0.5 * t * (1.0 + jnp.tanh(np.sqrt(2.0 / np.pi).astype(np.float32)
                                     * (t + 0.044715 * (t * t * t))))


def _ffn_kernel(x_ref, g2_ref, sc_ref, sh_ref, wg_ref, wu_ref, cw_ref, cb_ref, wd_ref, g3_ref,
                gt_ref, o_ref, h_s, acc_s, carry_s, *, per_batch):
    i = pl.program_id(0)
    j = pl.program_id(1)
    tm = x_ref.shape[0]

    @pl.when(j == 0)
    def _():
        h_s[...] = _modulated_norm(x_ref[...], g2_ref[...], sc_ref[0], sh_ref[0]).astype(BF16)

    h = h_s[...]
    seq_start = (i % per_batch) == 0
    off = pl.multiple_of(j * SUBLANES, SUBLANES)

    def conv(pre, half):
        prev = jnp.where(seq_start, 0.0, carry_s[half, pl.ds(off, SUBLANES), :])
        carry_s[half, pl.ds(off, SUBLANES), :] = pre[tm - SUBLANES:]
        cw = cw_ref[half]
        return (cw[0:1] * _shift_rows(pre, prev, 2) + cw[1:2] * _shift_rows(pre, prev, 1)
                + cw[2:3] * pre + cb_ref[half])

    gate = conv(jnp.dot(h, wg_ref[...], preferred_element_type=F32), 0)
    up = conv(jnp.dot(h, wu_ref[...], preferred_element_type=F32), 1)
    part = jnp.dot((_gelu_tanh(gate) * up).astype(BF16), wd_ref[...], preferred_element_type=F32)

    @pl.when(j == 0)
    def _():
        acc_s[...] = part

    @pl.when(j > 0)
    def _():
        acc_s[...] += part

    @pl.when(j == pl.num_programs(1) - 1)
    def _():
        o_ref[...] = x_ref[...] + gt_ref[0] * (_rms(acc_s[...], NORM_EPS) * g3_ref[...])


def conv_geglu_residual(x, g2, sc, sh, wg, wu, cw, cb, wd, g3, gt, seq):
    n, d = x.shape
    fp = wg.shape[1]
    tm = min(512, seq)
    tf = FFN_TILE
    nj = fp // tf
    per_batch = seq // tm
    mod = lambda i, j: (i // per_batch, 0, 0)
    return pl.pallas_call(
        functools.partial(_ffn_kernel, per_batch=per_batch),
        grid=(n // tm, nj),
        in_specs=[pl.BlockSpec((tm, d), lambda i, j: (i, 0)),
                  pl.BlockSpec((1, d), lambda i, j: (0, 0)),
                  pl.BlockSpec((1, 1, d), mod), pl.BlockSpec((1, 1, d), mod),
                  pl.BlockSpec((d, tf), lambda i, j: (0, j)),
                  pl.BlockSpec((d, tf), lambda i, j: (0, j)),
                  pl.BlockSpec((2, 3, tf), lambda i, j: (0, 0, j)),
                  pl.BlockSpec((2, 1, tf), lambda i, j: (0, 0, j)),
                  pl.BlockSpec((tf, d), lambda i, j: (j, 0)),
                  pl.BlockSpec((1, d), lambda i, j: (0, 0)),
                  pl.BlockSpec((1, 1, d), mod)],
        out_specs=pl.BlockSpec((tm, d), lambda i, j: (i, 0)),
        out_shape=jax.ShapeDtypeStruct((n, d), F32),
        scratch_shapes=[pltpu.VMEM((tm, d), BF16), pltpu.VMEM((tm, d), F32),
                        pltpu.VMEM((2, nj * SUBLANES, tf), F32)],
        compiler_params=_cparams(2),
        name="conv_geglu",
    )(x, g2, sc, sh, wg, wu, cw, cb, wd, g3, gt)


def _layer_params(l, w_in, rwkv_mu, rwkv_w2, rwkv_a2, rwkv_g2, w_gate, w_branch, w_out, w_up,
                  f_conv_w, f_conv_b, w_down):
    m_cols = BRANCH_DIM + 2048 + M_HEADS
    rw_cols = 3 * BRANCH_DIM + LORA_W + LORA_A + LORA_G
    wm, wr, wt = w_in[l, :, :m_cols], w_in[l, :, m_cols:m_cols + rw_cols], w_in[l, :, m_cols + rw_cols:]
    n_lora = LORA_W + LORA_A + LORA_G
    w_cat = jnp.concatenate([
        wm[:, :BRANCH_DIM + 2048], wr[:, :3 * BRANCH_DIM], wt,
        wr[:, 3 * BRANCH_DIM:], wm[:, BRANCH_DIM + 2048:],
        jnp.zeros((D_MODEL, TAIL - n_lora - M_HEADS), F32)], axis=1).astype(BF16)
    mu = rwkv_mu[l]
    mu_rkv = mu[:3 * BRANCH_DIM].reshape(1, -1)
    mu_tail = jnp.concatenate([mu[3 * BRANCH_DIM:], jnp.zeros((TAIL - n_lora,), F32)]).reshape(1, -1)

    def pad_rows(w, start):
        return jnp.zeros((TAIL, BRANCH_DIM), F32).at[start:start + w.shape[0]].set(w).astype(BF16)

    fpad = FFN_PAD - FFN_DIM
    wg = jnp.pad(w_up[l, :, :FFN_DIM], ((0, 0), (0, fpad))).astype(BF16)
    wu = jnp.pad(w_up[l, :, FFN_DIM:], ((0, 0), (0, fpad))).astype(BF16)
    cw = jnp.pad(jnp.stack([f_conv_w[l, :, :FFN_DIM], f_conv_w[l, :, FFN_DIM:]]),
                 ((0, 0), (0, 0), (0, fpad)))
    cb = jnp.pad(jnp.stack([f_conv_b[l, :FFN_DIM], f_conv_b[l, FFN_DIM:]]),
                 ((0, 0), (0, fpad)))[:, None, :]
    wd = jnp.pad(w_down[l], ((0, fpad), (0, 0))).astype(BF16)
    return dict(w_cat=w_cat, mu_rkv=mu_rkv, mu_tail=mu_tail,
                w2p=pad_rows(rwkv_w2[l], TAIL_WD), a2p=pad_rows(rwkv_a2[l], TAIL_AD),
                g2p=pad_rows(rwkv_g2[l], TAIL_GD),
                wgate=w_gate[l].astype(BF16), wbranch=w_branch[l].astype(BF16),
                wout=w_out[l].astype(BF16), wg=wg, wu=wu, cw=cw, cb=cb, wd=wd)


def kernel(x, c, positions, w_ada, b_ada, norm_g, w_in, m_conv_w, m_conv_b, m_head, m_norm_g,
           rwkv_mu, rwkv_w2, rwkv_a2, rwkv_g2, rwkv_vec, w_gate, w_branch, w_out, w_up, f_conv_w,
           f_conv_b, w_down):
    batch, seq, d = x.shape
    depth = w_ada.shape[0]
    n = batch * seq
    xf = x.reshape(n, d)
    pos_col = positions.reshape(n, 1)
    c_pad = jnp.zeros((SUBLANES, d), F32).at[:batch].set(c)
    for l in range(depth):
        p = _layer_params(l, w_in, rwkv_mu, rwkv_w2, rwkv_a2, rwkv_g2, w_gate, w_branch, w_out,
                          w_up, f_conv_w, f_conv_b, w_down)
        mod = ada_mod(c_pad, w_ada[l], b_ada[l])[:batch].reshape(batch, 6, 1, d)
        sh_m, sc_m, gt_m, sh_f, sc_f, gt_f = [mod[:, i] for i in range(6)]
        g = norm_g[l].reshape(4, 1, d)
        proj, h = norm_proj(xf, g[0], sc_m, sh_m, p["w_cat"], seq)
        y_ssd = ssd_branch(proj, m_conv_w[l], m_conv_b[l], m_head[l], m_norm_g[l], batch, seq)
        y_rwkv = rwkv_branch(proj, p["mu_rkv"], p["mu_tail"], p["w2p"], p["a2p"], p["g2p"],
                             rwkv_vec[l], batch, seq)
        y_ret = ret_branch(proj, pos_col, batch, seq)
        merged = merge_branches(h, (y_ssd, y_rwkv, y_ret), p["wgate"], p["wbranch"], seq)
        xf = proj_residual(merged, p["wout"], xf, g[1], gt_m, seq)
        xf = conv_geglu_residual(xf, g[2], sc_f, sh_f, p["wg"], p["wu"], p["cw"], p["cb"], p["wd"],
                                 g[3], gt_f, seq)
    return xf.reshape(batch, seq, d)
```

```python
import functools

import numpy as np
import jax
import jax.numpy as jnp
from jax import lax
from jax.experimental import pallas as pl
from jax.experimental.pallas import tpu as pltpu

F32 = jnp.float32
BF16 = jnp.bfloat16
HIGHEST = lax.Precision.HIGHEST

D_MODEL = 2048
HEAD_DIM = 64
BRANCH_DIM = D_MODEL // 2
SSD_CHUNK = 128
M_HEADS = 16
M_GROUPS = 4
M_STATE = 128
M_CONV = 4
M_NORM_EPS = 1e-5
RW_CHUNK = 64
RW_LN_EPS = 64e-5
LORA_W, LORA_A, LORA_G = 96, 96, 256
RET_HEADS = 8
RET_DK = 64
RET_EPS = 1e-6
ROPE_BASE = 10000.0
FFN_DIM = 5504
FFN_PAD = 5632
FFN_TILE = 512
NORM_EPS = 1e-6
LANES = 128
SUBLANES = 8
TAIL = 512
VMEM_LIMIT = 52 * 1024 * 1024

COL_Z, COL_XS, COL_BC = 0, 1024, 2048
COL_RKV = 3072
COL_RQ, COL_RK, COL_RV, COL_RG = 6144, 6656, 7168, 8192
COL_TAIL = 9216
PROJ_COLS = COL_TAIL + TAIL
TAIL_WD, TAIL_AD, TAIL_GD, TAIL_DT = 0, 96, 192, 448


def _cparams(n_axes):
    return pltpu.CompilerParams(dimension_semantics=("arbitrary",) * n_axes,
                                vmem_limit_bytes=VMEM_LIMIT)


def _bdot(a, b):
    return jnp.dot(a.astype(BF16), b.astype(BF16), preferred_element_type=F32)


def _bdot_nt(a, b):
    return lax.dot_general(a.astype(BF16), b.astype(BF16), (((1,), (1,)), ((), ())),
                           preferred_element_type=F32)


def _bdot_tn(a, b):
    return lax.dot_general(a.astype(BF16), b.astype(BF16), (((0,), (0,)), ((), ())),
                           preferred_element_type=F32)


def _hdot(a, b):
    return jnp.dot(a, b, preferred_element_type=F32, precision=HIGHEST)


def _hdot_nt(a, b):
    return lax.dot_general(a, b, (((1,), (1,)), ((), ())), preferred_element_type=F32,
                           precision=HIGHEST)


def _split3(t):
    hi = t.astype(BF16)
    rest = t - hi.astype(F32)
    mid = rest.astype(BF16)
    lo = (rest - mid.astype(F32)).astype(BF16)
    return hi, mid, lo


def _silu(t):
    return t * jax.nn.sigmoid(t)


def _softplus(t):
    return jnp.maximum(t, 0.0) + jnp.log1p(jnp.exp(-jnp.abs(t)))


def _rms(t, eps):
    return t * lax.rsqrt(jnp.mean(t * t, axis=-1, keepdims=True) + eps)


def _shift_rows(cur, prev, k):
    rolled = pltpu.roll(cur, k, 0)
    prev8 = pltpu.roll(prev[prev.shape[0] - SUBLANES:], k, 0)
    row = lax.broadcasted_iota(jnp.int32, (SUBLANES, cur.shape[1]), 0)
    top = jnp.where(row < k, prev8, rolled[:SUBLANES])
    return jnp.concatenate([top, rolled[SUBLANES:]], axis=0)


def _ada_kernel(c_ref, w_ref, b_ref, o_ref):
    o_ref[...] = _bdot(_silu(c_ref[...]), w_ref[...]) + b_ref[...]


def ada_mod(c_pad, w, b):
    rows, d = c_pad.shape
    cols = w.shape[1]
    tn = 1024
    return pl.pallas_call(
        _ada_kernel,
        grid=(cols // tn,),
        in_specs=[pl.BlockSpec((rows, d), lambda j: (0, 0)),
                  pl.BlockSpec((d, tn), lambda j: (0, j)),
                  pl.BlockSpec((1, tn), lambda j: (0, j))],
        out_specs=pl.BlockSpec((rows, tn), lambda j: (0, j)),
        out_shape=jax.ShapeDtypeStruct((rows, cols), F32),
        compiler_params=_cparams(1),
        name="ada_mod",
    )(c_pad, w, b.reshape(1, cols))


def _modulated_norm(x, g, sc, sh):
    return _rms(x, NORM_EPS) * g * (1.0 + sc) + sh


def _norm_proj_kernel(x_ref, g_ref, sc_ref, sh_ref, w_ref, o_ref, h_ref):
    @pl.when(pl.program_id(1) == 0)
    def _():
        h_ref[...] = _modulated_norm(x_ref[...], g_ref[...], sc_ref[0], sh_ref[0]).astype(BF16)

    o_ref[...] = jnp.dot(h_ref[...], w_ref[...], preferred_element_type=F32)


def norm_proj(x, g, sc, sh, w, seq):
    n, d = x.shape
    cols = w.shape[1]
    tm = min(512, seq)
    tn = cols // 4
    per_batch = seq // tm
    return pl.pallas_call(
        _norm_proj_kernel,
        grid=(n // tm, cols // tn),
        in_specs=[pl.BlockSpec((tm, d), lambda i, j: (i, 0)),
                  pl.BlockSpec((1, d), lambda i, j: (0, 0)),
                  pl.BlockSpec((1, 1, d), lambda i, j: (i // per_batch, 0, 0)),
                  pl.BlockSpec((1, 1, d), lambda i, j: (i // per_batch, 0, 0)),
                  pl.BlockSpec((d, tn), lambda i, j: (0, j))],
        out_specs=[pl.BlockSpec((tm, tn), lambda i, j: (i, j)),
                   pl.BlockSpec((tm, d), lambda i, j: (i, 0))],
        out_shape=[jax.ShapeDtypeStruct((n, cols), F32),
                   jax.ShapeDtypeStruct((n, d), BF16)],
        compiler_params=_cparams(2),
        name="norm_proj",
    )(x, g, sc, sh, w)


def _ssd_kernel(z_ref, xs_ref, bc_ref, tail_ref, cw_ref, cb_ref, dtb_ref, alog_ref, dskip_ref,
                ng_ref, expand_ref, o_ref, prev_s, state_s):
    L = SSD_CHUNK

    @pl.when(pl.program_id(1) == 0)
    def _():
        prev_s[...] = jnp.zeros_like(prev_s)
        state_s[...] = jnp.zeros_like(state_s)

    xbc = jnp.concatenate([xs_ref[...], bc_ref[...]], axis=1)
    prev = prev_s[...]
    prev_s[...] = xbc
    cw = cw_ref[...]
    acc = cw[M_CONV - 1:M_CONV] * xbc + cb_ref[...]
    for k in range(1, M_CONV):
        acc = acc + cw[M_CONV - 1 - k:M_CONV - k] * _shift_rows(xbc, prev, k)
    xc = _silu(acc)
    xs = xc[:, :BRANCH_DIM]
    bm = xc[:, BRANCH_DIM:BRANCH_DIM + M_GROUPS * M_STATE]
    cm = xc[:, BRANCH_DIM + M_GROUPS * M_STATE:]

    dt = _softplus(tail_ref[:, TAIL_DT:TAIL_DT + M_HEADS] + dtb_ref[...])
    a_dt = dt * -jnp.exp(alog_ref[...])
    ri = lax.broadcasted_iota(jnp.int32, (L, L), 0)
    ci = lax.broadcasted_iota(jnp.int32, (L, L), 1)
    causal = ri >= ci
    tri = causal.astype(F32)
    cum = _hdot(tri, a_dt)
    eye_h = (lax.broadcasted_iota(jnp.int32, (M_HEADS, M_HEADS), 0)
             == lax.broadcasted_iota(jnp.int32, (M_HEADS, M_HEADS), 1)).astype(F32)
    cum_t = _hdot_nt(eye_h, cum)

    expand = expand_ref[...]
    dt_full = _hdot(dt, expand)
    cum_full = _hdot(cum, expand)
    cum_last = cum_full[L - 1:L, :]
    xdt = xs * dt_full
    ecum = jnp.exp(cum_full)
    to_end = jnp.exp(cum_last - cum_full)
    elast = jnp.exp(cum_last)

    lane = lax.broadcasted_iota(jnp.int32, (L, LANES), 1)
    first = lane < HEAD_DIM
    ys = []
    for p in range(M_HEADS // 2):
        g = p // 2
        lo = p * LANES
        bg = bm[:, g * M_STATE:(g + 1) * M_STATE]
        cg = cm[:, g * M_STATE:(g + 1) * M_STATE]
        cb = _bdot_nt(cg, bg)
        ms = []
        for h in (2 * p, 2 * p + 1):
            seg = cum[:, h:h + 1] - cum_t[h:h + 1, :]
            ms.append(cb * jnp.exp(jnp.where(causal, seg, -1e30)))
        xp = xdt[:, lo:lo + LANES]
        x2 = jnp.concatenate([jnp.where(first, xp, 0.0), jnp.where(first, 0.0, xp)], axis=0)
        y = _bdot(jnp.concatenate(ms, axis=1), x2)
        st = state_s[p]
        y = y + _bdot(cg, st) * ecum[:, lo:lo + LANES]
        state_s[p] = st * elast[:, lo:lo + LANES] + _bdot_tn(bg, xp * to_end[:, lo:lo + LANES])
        ys.append(y)
    y = jnp.concatenate(ys, axis=1) + xs * dskip_ref[...]
    y = y * _silu(z_ref[...])
    gw = BRANCH_DIM // M_GROUPS
    y = jnp.concatenate([_rms(y[:, g * gw:(g + 1) * gw], M_NORM_EPS) for g in range(M_GROUPS)],
                        axis=1)
    o_ref[...] = (y * ng_ref[...]).astype(BF16)


def ssd_branch(proj, conv_w, conv_b, head_p, norm_g, batch, seq):
    n = proj.shape[0]
    L = SSD_CHUNK
    nc = seq // L
    expand = jnp.repeat(jnp.eye(M_HEADS, dtype=F32), HEAD_DIM, axis=1)
    dskip = jnp.repeat(head_p[2], HEAD_DIM).reshape(1, BRANCH_DIM)

    def rows(width, idx):
        return pl.BlockSpec((L, width), lambda b, c: (b * nc + c, idx))

    def const(shape):
        return pl.BlockSpec(shape, lambda b, c: (0,) * len(shape))

    return pl.pallas_call(
        _ssd_kernel,
        grid=(batch, nc),
        in_specs=[rows(1024, COL_Z // 1024), rows(1024, COL_XS // 1024), rows(1024, COL_BC // 1024),
                  rows(TAIL, COL_TAIL // TAIL),
                  const((M_CONV, 2048)), const((1, 2048)), const((1, M_HEADS)), const((1, M_HEADS)),
                  const((1, BRANCH_DIM)), const((1, BRANCH_DIM)), const((M_HEADS, BRANCH_DIM))],
        out_specs=rows(BRANCH_DIM, 0),
        out_shape=jax.ShapeDtypeStruct((n, BRANCH_DIM), BF16),
        scratch_shapes=[pltpu.VMEM((L, 2048), F32),
                        pltpu.VMEM((M_HEADS // 2, M_STATE, LANES), F32)],
        compiler_params=_cparams(2),
        name="ssd_branch",
    )(proj, proj, proj, proj, conv_w, conv_b.reshape(1, -1), head_p[0].reshape(1, -1),
      head_p[1].reshape(1, -1), dskip, norm_g.reshape(1, -1), expand)


def _rwkv_kernel(rkv_ref, tail_ref, mu_rkv_ref, mu_tail_ref, w2_ref, a2_ref, g2_ref, vec_ref,
                 o_ref, prev_rkv_s, prev_tail_s, state_s):
    C = RW_CHUNK
    R = 2 * C
    nb = rkv_ref.shape[0]
    n_pairs = BRANCH_DIM // LANES
    units = [(b, p) for b in range(nb) for p in range(n_pairs)]
    n_u = len(units)

    @pl.when(pl.program_id(0) == 0)
    def _():
        prev_rkv_s[...] = jnp.zeros_like(prev_rkv_s)
        prev_tail_s[...] = jnp.zeros_like(prev_tail_s)
        state_s[...] = jnp.zeros_like(state_s)

    def token_shift(ref, prev_s, mu):
        parts = []
        for b in range(nb):
            raw = ref[b]
            parts.append(raw + (_shift_rows(raw, prev_s[b], 1) - raw) * mu)
            prev_s[b] = raw[C - SUBLANES:]
        return jnp.concatenate(parts, axis=0)

    rkv = token_shift(rkv_ref, prev_rkv_s, mu_rkv_ref[...])
    tail = token_shift(tail_ref, prev_tail_s, mu_tail_ref[...])
    r = rkv[:, :BRANCH_DIM]
    k = rkv[:, BRANCH_DIM:2 * BRANCH_DIM]
    v = rkv[:, 2 * BRANCH_DIM:]

    vec = vec_ref[...]
    w0, a0, k_k, k_a, r_k, ln_w, ln_b = [vec[i:i + 1] for i in range(7)]
    log_w = -_softplus(-(w0 + _bdot(jnp.tanh(tail), w2_ref[...]))) - 0.5
    neg_log_decay = jnp.exp(log_w)
    a = jax.nn.sigmoid(a0 + _bdot(tail, a2_ref[...]))
    g = _bdot(jax.nn.sigmoid(tail), g2_ref[...])

    lane = lax.broadcasted_iota(jnp.int32, (R, R), 1)
    sub = lax.broadcasted_iota(jnp.int32, (R, R), 0)
    same_head = (lane >> 6) == (sub >> 6)
    head_ones3 = jnp.concatenate([same_head.astype(BF16)] * 3, axis=0)
    strict = same_head & (sub > lane)
    incl = same_head & (sub >= lane)
    eye = (sub == lane).astype(F32)
    first = lax.broadcasted_iota(jnp.int32, (C, LANES), 1) < HEAD_DIM

    def head_sum(t):
        return jnp.dot(jnp.concatenate(_split3(t), axis=1), head_ones3, preferred_element_type=F32)

    rows = nb * C
    ri = lax.broadcasted_iota(jnp.int32, (rows, rows), 0)
    ci = lax.broadcasted_iota(jnp.int32, (rows, rows), 1)
    tri3 = jnp.concatenate([(((ri >> 6) == (ci >> 6)) & (ri >= ci)).astype(BF16)] * 3, axis=1)
    cl = -jnp.dot(tri3, jnp.concatenate(_split3(neg_log_decay), axis=0), preferred_element_type=F32)

    kk = k * k_k
    k2 = k * (1.0 + (a - 1.0) * k_a)

    def unit(t, b, p):
        return t[b * C:(b + 1) * C, p * LANES:(p + 1) * LANES]

    def lanes(t, p):
        return t[:, p * LANES:(p + 1) * LANES]

    def stack(t):
        return jnp.concatenate([jnp.where(first, t, 0.0), jnp.where(first, 0.0, t)], axis=0)

    def dup(t):
        return jnp.concatenate([t, t], axis=0)

    kk_u = [unit(kk, b, p) for b, p in units]
    k2_u = [unit(k2, b, p) for b, p in units]
    r_u = [unit(r, b, p) for b, p in units]
    v_u = [unit(v, b, p) for b, p in units]
    cl_u = [unit(cl, b, p) for b, p in units]
    nld_u = [unit(neg_log_decay, b, p) for b, p in units]
    a_u = [unit(a, b, p) for b, p in units]

    sums = head_sum(jnp.concatenate(
        [t * t for t in kk_u] + [r_u[i] * k2_u[i] * lanes(r_k, units[i][1]) for i in range(n_u)],
        axis=0))
    kk_u = [kk_u[i] * lax.rsqrt(jnp.maximum(sums[i * C:(i + 1) * C], 1e-24)) for i in range(n_u)]
    bonus_u = [sums[(n_u + i) * C:(n_u + i + 1) * C] * v_u[i] for i in range(n_u)]
    b_u = [kk_u[i] * a_u[i] for i in range(n_u)]

    w_inv = [jnp.exp(-t) for t in cl_u]
    lhs_u = [jnp.concatenate([stack(kk_u[i] * jnp.exp(cl_u[i] + nld_u[i])),
                              stack(r_u[i] * jnp.exp(cl_u[i]))], axis=0).astype(BF16)
             for i in range(n_u)]
    rhs_u = [jnp.concatenate([dup(b_u[i] * w_inv[i]), dup(k2_u[i] * w_inv[i])], axis=0).astype(BF16)
             for i in range(n_u)]
    gram_u = [lax.dot_general(lhs_u[i], rhs_u[i], (((1,), (1,)), ((), ())),
                              preferred_element_type=F32) for i in range(n_u)]
    a_b = [jnp.where(strict, t[:R, :R], 0.0) for t in gram_u]
    a_k = [jnp.where(strict, t[:R, R:], 0.0).astype(BF16) for t in gram_u]
    l_bk = [jnp.concatenate([jnp.where(incl, t[R:, :R], 0.0), jnp.where(incl, t[R:, R:], 0.0)],
                            axis=1).astype(BF16) for t in gram_u]
    minv = [eye - t for t in a_b]
    pw = [t.astype(BF16) for t in a_b]
    for _ in range(int(np.log2(C)) - 1):
        pw = [jnp.dot(t, t, preferred_element_type=F32).astype(BF16) for t in pw]
        minv = [minv[i] + jnp.dot(minv[i].astype(BF16), pw[i], preferred_element_type=F32)
                for i in range(n_u)]
    minv = [t.astype(BF16) for t in minv]

    vs = [stack(t).astype(BF16) for t in v_u]
    akv = [jnp.dot(a_k[i], vs[i], preferred_element_type=F32) for i in range(n_u)]
    st = [state_s[i] for i in range(n_u)]
    sp = [_bdot_nt(lhs_u[i], st[i]) for i in range(n_u)]
    u = [jnp.dot(minv[i], (-sp[i][:R] - akv[i]).astype(BF16), preferred_element_type=F32)
         for i in range(n_u)]
    uv = [jnp.concatenate([u[i].astype(BF16), vs[i]], axis=0) for i in range(n_u)]
    ysk = [sp[i][R:] + jnp.dot(l_bk[i], uv[i], preferred_element_type=F32) for i in range(n_u)]
    y_all = jnp.concatenate([t[:C] + t[C:] for t in ysk], axis=0)
    for i in range(n_u):
        cl_last = cl_u[i][C - 1:C]
        w_end = jnp.exp(cl_last - cl_u[i])
        ends = jnp.concatenate([stack(b_u[i] * w_end), stack(k2_u[i] * w_end)], axis=0)
        state_s[i] = st[i] * jnp.exp(cl_last) + _bdot_tn(uv[i], ends)

    dev = y_all - head_sum(y_all) * (1.0 / HEAD_DIM)
    var = head_sum(dev * dev) * (1.0 / HEAD_DIM)
    yn = dev * lax.rsqrt(var + RW_LN_EPS)
    for b in range(nb):
        outs = []
        for p in range(n_pairs):
            i = b * n_pairs + p
            y = yn[i * C:(i + 1) * C] * lanes(ln_w, p) + lanes(ln_b, p)
            outs.append((y + bonus_u[i]) * unit(g, b, p))
        o_ref[b] = jnp.concatenate(outs, axis=1).astype(BF16)


def rwkv_branch(proj, mu_rkv, mu_tail, w2p, a2p, g2p, vec, batch, seq):
    n = proj.shape[0]
    C = RW_CHUNK
    proj3 = proj.reshape(batch, seq, PROJ_COLS)

    def rows(width, idx):
        return pl.BlockSpec((batch, C, width), lambda c: (0, c, idx))

    def const(shape):
        return pl.BlockSpec(shape, lambda c: (0,) * len(shape))

    out = pl.pallas_call(
        _rwkv_kernel,
        grid=(seq // C,),
        in_specs=[rows(3 * BRANCH_DIM, COL_RKV // (3 * BRANCH_DIM)), rows(TAIL, COL_TAIL // TAIL),
                  const((1, 3 * BRANCH_DIM)), const((1, TAIL)),
                  const((TAIL, BRANCH_DIM)), const((TAIL, BRANCH_DIM)), const((TAIL, BRANCH_DIM)),
                  const((7, BRANCH_DIM))],
        out_specs=rows(BRANCH_DIM, 0),
        out_shape=jax.ShapeDtypeStruct((batch, seq, BRANCH_DIM), BF16),
        scratch_shapes=[pltpu.VMEM((batch, SUBLANES, 3 * BRANCH_DIM), F32),
                        pltpu.VMEM((batch, SUBLANES, TAIL), F32),
                        pltpu.VMEM((batch * BRANCH_DIM // LANES, LANES, LANES), F32)],
        compiler_params=_cparams(1),
        name="rwkv_branch",
    )(proj3, proj3, mu_rkv, mu_tail, w2p, a2p, g2p, vec)
    return out.reshape(n, BRANCH_DIM)


def _ret_kernel(q_ref, k_ref, v_ref, g_ref, pos_ref, invf_ref, dmask_ref, qdec_ref, kdec_ref,
                cdec_ref, o_ref, state_s):
    L = SSD_CHUNK
    half = RET_DK // 2
    width = RET_HEADS * RET_DK

    @pl.when(pl.program_id(1) == 0)
    def _():
        state_s[...] = jnp.zeros_like(state_s)

    ang = pos_ref[...].astype(F32) * invf_ref[...]
    cos = jnp.concatenate([jnp.cos(ang)] * (width // LANES), axis=1)
    sin = jnp.concatenate([jnp.sin(ang)] * (width // LANES), axis=1)
    lane_w = lax.broadcasted_iota(jnp.int32, (L, width), 1)
    low_half = (lane_w & (RET_DK - 1)) < half

    def rope(t):
        rot = jnp.where(low_half, -pltpu.roll(t, width - half, 1), pltpu.roll(t, half, 1))
        return t * cos + rot * sin

    q = rope(q_ref[...])
    k = rope(k_ref[...]) * (RET_DK ** -0.5)
    qd = q * qdec_ref[...]
    kd = k * kdec_ref[...]
    v = v_ref[...]
    g = g_ref[...]
    lane = lax.broadcasted_iota(jnp.int32, (L, LANES), 1)
    outs = []
    for h in range(RET_HEADS):
        lo = (h // 2) * LANES
        mine = (lane >> 6) == (h % 2)
        qh = jnp.where(mine, q[:, lo:lo + LANES], 0.0)
        qdh = jnp.where(mine, qd[:, lo:lo + LANES], 0.0)
        kdh = jnp.where(mine, kd[:, lo:lo + LANES], 0.0)
        vh = v[:, h * LANES:(h + 1) * LANES]
        scores = _bdot_nt(qh, k[:, lo:lo + LANES]) * dmask_ref[h]
        st = state_s[h]
        y = _bdot(scores, vh) + _bdot(qdh, st)
        state_s[h] = st * cdec_ref[h] + _bdot_tn(kdh, vh)
        outs.append(_rms(y, RET_EPS))
    o_ref[...] = (_silu(g) * jnp.concatenate(outs, axis=1)).astype(BF16)


def ret_branch(proj, pos_col, batch, seq):
    n = proj.shape[0]
    L = SSD_CHUNK
    nc = seq // L
    half = RET_DK // 2
    inv_freq = ROPE_BASE ** (-jnp.arange(half, dtype=F32) / half)
    invf = jnp.tile(inv_freq, LANES // half).reshape(1, LANES)
    log_gamma = jnp.log1p(-jnp.exp2(-5.0 - jnp.arange(RET_HEADS, dtype=F32)))
    idx = jnp.arange(L, dtype=F32)
    rel = (idx[:, None] - idx[None, :])[None]
    dmask = jnp.exp(jnp.where(rel >= 0, rel * log_gamma[:, None, None], -jnp.inf))
    qdec = jnp.repeat(jnp.exp((idx[:, None] + 1.0) * log_gamma), RET_DK, axis=1)
    kdec = jnp.repeat(jnp.exp((L - 1.0 - idx)[:, None] * log_gamma), RET_DK, axis=1)
    cdec = jnp.broadcast_to(jnp.exp(L * log_gamma)[:, None, None], (RET_HEADS, 1, LANES))

    def rows(width, idx_):
        return pl.BlockSpec((L, width), lambda b, c: (b * nc + c, idx_))

    def const(shape):
        return pl.BlockSpec(shape, lambda b, c: (0,) * len(shape))

    return pl.pallas_call(
        _ret_kernel,
        grid=(batch, nc),
        in_specs=[rows(512, COL_RQ // 512), rows(512, COL_RK // 512), rows(1024, COL_RV // 1024),
                  rows(1024, COL_RG // 1024), rows(1, 0),
                  const((1, LANES)), const((RET_HEADS, L, L)), const((L, 512)), const((L, 512)),
                  const((RET_HEADS, 1, LANES))],
        out_specs=rows(BRANCH_DIM, 0),
        out_shape=jax.ShapeDtypeStruct((n, BRANCH_DIM), BF16),
        scratch_shapes=[pltpu.VMEM((RET_HEADS, LANES, LANES), F32)],
        compiler_params=_cparams(2),
        name="ret_branch",
    )(proj, proj, proj, proj, pos_col, invf, dmask, qdec, kdec, cdec)


def _merge_kernel(h_ref, y0_ref, y1_ref, y2_ref, wg_ref, wb_ref, o_ref):
    h = h_ref[...]
    acc = None
    for b, y_ref in enumerate((y0_ref, y1_ref, y2_ref)):
        gate = jax.nn.sigmoid(jnp.dot(h, wg_ref[b], preferred_element_type=F32))
        term = gate * jnp.dot(y_ref[...], wb_ref[b], preferred_element_type=F32)
        acc = term if acc is None else acc + term
    o_ref[...] = acc.astype(BF16)


def merge_branches(h, ys, wg, wb, seq):
    n, d = h.shape
    tm = min(1024, seq)
    tn = 512
    yspec = pl.BlockSpec((tm, BRANCH_DIM), lambda i, j: (i, 0))
    return pl.pallas_call(
        _merge_kernel,
        grid=(n // tm, d // tn),
        in_specs=[pl.BlockSpec((tm, d), lambda i, j: (i, 0)), yspec, yspec, yspec,
                  pl.BlockSpec((3, d, tn), lambda i, j: (0, 0, j)),
                  pl.BlockSpec((3, BRANCH_DIM, tn), lambda i, j: (0, 0, j))],
        out_specs=pl.BlockSpec((tm, tn), lambda i, j: (i, j)),
        out_shape=jax.ShapeDtypeStruct((n, d), BF16),
        compiler_params=_cparams(2),
        name="merge_branches",
    )(h, *ys, wg, wb)


def _proj_residual_kernel(a_ref, w_ref, x_ref, g_ref, gt_ref, o_ref):
    y = jnp.dot(a_ref[...], w_ref[...], preferred_element_type=F32)
    o_ref[...] = x_ref[...] + gt_ref[0] * (_rms(y, NORM_EPS) * g_ref[...])


def proj_residual(a, w, x, g, gt, seq):
    n, d = x.shape
    tm = min(512, seq)
    per_batch = seq // tm
    return pl.pallas_call(
        _proj_residual_kernel,
        grid=(n // tm,),
        in_specs=[pl.BlockSpec((tm, d), lambda i: (i, 0)),
                  pl.BlockSpec((d, d), lambda i: (0, 0)),
                  pl.BlockSpec((tm, d), lambda i: (i, 0)),
                  pl.BlockSpec((1, d), lambda i: (0, 0)),
                  pl.BlockSpec((1, 1, d), lambda i: (i // per_batch, 0, 0))],
        out_specs=pl.BlockSpec((tm, d), lambda i: (i, 0)),
        out_shape=jax.ShapeDtypeStruct((n, d), F32),
        compiler_params=_cparams(1),
        name="proj_residual",
    )(a, w, x, g, gt)


def _gelu_tanh(t):
    return 0.5 * t * (1.0 + jnp.tanh(np.sqrt(2.0 / np.pi).astype(np.float32)
                                     * (t + 0.044715 * (t * t * t))))


def _ffn_kernel(x_ref, g2_ref, sc_ref, sh_ref, wg_ref, wu_ref, cw_ref, cb_ref, wd_ref, g3_ref,
                gt_ref, o_ref, h_s, acc_s, carry_s, pre_even_s, pre_odd_s, *, per_batch, nj):
    i = pl.program_id(0)
    s = pl.program_id(1)
    tm = x_ref.shape[0]
    seq_start = (i % per_batch) == 0

    n_chunks = 4
    rc = tm // n_chunks

    tf = wg_ref.shape[1]

    def up_proj_piece(pre_s, c):
        half, cols = c // 2, slice((c % 2) * (tf // 2), (c % 2 + 1) * (tf // 2))
        w_ref = wu_ref if half else wg_ref
        pre_s[half, :, cols] = jnp.dot(h_s[...], w_ref[:, cols], preferred_element_type=F32)

    def up_proj(pre_s):
        for c in range(n_chunks):
            up_proj_piece(pre_s, c)

    def conv(pre_s, half, c):
        off = pl.multiple_of((s - 1) * SUBLANES, SUBLANES)
        pre = pre_s[half, c * rc:(c + 1) * rc, :]
        if c == 0:
            prev = jnp.where(seq_start, 0.0, carry_s[half, pl.ds(off, SUBLANES), :])
        else:
            prev = pre_s[half, c * rc - SUBLANES:c * rc, :]
        if c == n_chunks - 1:
            carry_s[half, pl.ds(off, SUBLANES), :] = pre[rc - SUBLANES:]
        cw = cw_ref[half]
        return (cw[0:1] * _shift_rows(pre, prev, 2) + cw[1:2] * _shift_rows(pre, prev, 1)
                + cw[2:3] * pre + cb_ref[half])

    def down_proj(pre_s, pre_next_s=None):
        for c in range(n_chunks):
            if pre_next_s is not None:
                up_proj_piece(pre_next_s, c)
            act = (_gelu_tanh(conv(pre_s, 0, c)) * conv(pre_s, 1, c)).astype(BF16)
            acc_s[c * rc:(c + 1) * rc, :] += jnp.dot(act, wd_ref[...], preferred_element_type=F32)

    @pl.when(s == 0)
    def _():
        h_s[...] = _modulated_norm(x_ref[...], g2_ref[...], sc_ref[0], sh_ref[0]).astype(BF16)
        acc_s[...] = jnp.zeros_like(acc_s)
        up_proj(pre_even_s)

    @pl.when((s > 0) & (s < nj) & (s % 2 == 1))
    def _():
        down_proj(pre_even_s, pre_odd_s)

    @pl.when((s > 0) & (s < nj) & (s % 2 == 0))
    def _():
        down_proj(pre_odd_s, pre_even_s)

    @pl.when(s == nj)
    def _():
        down_proj(pre_even_s if (nj - 1) % 2 == 0 else pre_odd_s)
        o_ref[...] = x_ref[...] + gt_ref[0] * (_rms(acc_s[...], NORM_EPS) * g3_ref[...])


def conv_geglu_residual(x, g2, sc, sh, wg, wu, cw, cb, wd, g3, gt, seq):
    n, d = x.shape
    fp = wg.shape[1]
    tm = min(512, seq)
    tf = FFN_TILE
    nj = fp // tf
    per_batch = seq // tm
    mod = lambda i, s: (i // per_batch, 0, 0)
    ahead = lambda i, s: (0, jnp.minimum(s, nj - 1))
    behind = lambda i, s: (0, 0, jnp.maximum(s - 1, 0))
    return pl.pallas_call(
        functools.partial(_ffn_kernel, per_batch=per_batch, nj=nj),
        grid=(n // tm, nj + 1),
        in_specs=[pl.BlockSpec((tm, d), lambda i, s: (i, 0)),
                  pl.BlockSpec((1, d), lambda i, s: (0, 0)),
                  pl.BlockSpec((1, 1, d), mod), pl.BlockSpec((1, 1, d), mod),
                  pl.BlockSpec((d, tf), ahead),
                  pl.BlockSpec((d, tf), ahead),
                  pl.BlockSpec((2, 3, tf), behind),
                  pl.BlockSpec((2, 1, tf), behind),
                  pl.BlockSpec((tf, d), lambda i, s: (jnp.maximum(s - 1, 0), 0)),
                  pl.BlockSpec((1, d), lambda i, s: (0, 0)),
                  pl.BlockSpec((1, 1, d), mod)],
        out_specs=pl.BlockSpec((tm, d), lambda i, s: (i, 0)),
        out_shape=jax.ShapeDtypeStruct((n, d), F32),
        scratch_shapes=[pltpu.VMEM((tm, d), BF16), pltpu.VMEM((tm, d), F32),
                        pltpu.VMEM((2, nj * SUBLANES, tf), F32),
                        pltpu.VMEM((2, tm, tf), F32), pltpu.VMEM((2, tm, tf), F32)],
        compiler_params=_cparams(2),
        name="conv_geglu",
    )(x, g2, sc, sh, wg, wu, cw, cb, wd, g3, gt)


def _layer_params(l, w_in, rwkv_mu, rwkv_w2, rwkv_a2, rwkv_g2, w_gate, w_branch, w_out, w_up,
                  f_conv_w, f_conv_b, w_down):
    m_cols = BRANCH_DIM + 2048 + M_HEADS
    rw_cols = 3 * BRANCH_DIM + LORA_W + LORA_A + LORA_G
    wm, wr, wt = w_in[l, :, :m_cols], w_in[l, :, m_cols:m_cols + rw_cols], w_in[l, :, m_cols + rw_cols:]
    n_lora = LORA_W + LORA_A + LORA_G
    w_cat = jnp.concatenate([
        wm[:, :BRANCH_DIM + 2048], wr[:, :3 * BRANCH_DIM], wt,
        wr[:, 3 * BRANCH_DIM:], wm[:, BRANCH_DIM + 2048:],
        jnp.zeros((D_MODEL, TAIL - n_lora - M_HEADS), F32)], axis=1).astype(BF16)
    mu = rwkv_mu[l]
    mu_rkv = mu[:3 * BRANCH_DIM].reshape(1, -1)
    mu_tail = jnp.concatenate([mu[3 * BRANCH_DIM:], jnp.zeros((TAIL - n_lora,), F32)]).reshape(1, -1)

    def pad_rows(w, start):
        return jnp.zeros((TAIL, BRANCH_DIM), F32).at[start:start + w.shape[0]].set(w).astype(BF16)

    fpad = FFN_PAD - FFN_DIM
    wg = jnp.pad(w_up[l, :, :FFN_DIM], ((0, 0), (0, fpad))).astype(BF16)
    wu = jnp.pad(w_up[l, :, FFN_DIM:], ((0, 0), (0, fpad))).astype(BF16)
    cw = jnp.pad(jnp.stack([f_conv_w[l, :, :FFN_DIM], f_conv_w[l, :, FFN_DIM:]]),
                 ((0, 0), (0, 0), (0, fpad)))
    cb = jnp.pad(jnp.stack([f_conv_b[l, :FFN_DIM], f_conv_b[l, FFN_DIM:]]),
                 ((0, 0), (0, fpad)))[:, None, :]
    wd = jnp.pad(w_down[l], ((0, fpad), (0, 0))).astype(BF16)
    return dict(w_cat=w_cat, mu_rkv=mu_rkv, mu_tail=mu_tail,
                w2p=pad_rows(rwkv_w2[l], TAIL_WD), a2p=pad_rows(rwkv_a2[l], TAIL_AD),
                g2p=pad_rows(rwkv_g2[l], TAIL_GD),
                wgate=w_gate[l].astype(BF16), wbranch=w_branch[l].astype(BF16),
                wout=w_out[l].astype(BF16), wg=wg, wu=wu, cw=cw, cb=cb, wd=wd)


def kernel(x, c, positions, w_ada, b_ada, norm_g, w_in, m_conv_w, m_conv_b, m_head, m_norm_g,
           rwkv_mu, rwkv_w2, rwkv_a2, rwkv_g2, rwkv_vec, w_gate, w_branch, w_out, w_up, f_conv_w,
           f_conv_b, w_down):
    batch, seq, d = x.shape
    depth = w_ada.shape[0]
    n = batch * seq
    xf = x.reshape(n, d)
    pos_col = positions.reshape(n, 1)
    c_pad = jnp.zeros((SUBLANES, d), F32).at[:batch].set(c)
    for l in range(depth):
        p = _layer_params(l, w_in, rwkv_mu, rwkv_w2, rwkv_a2, rwkv_g2, w_gate, w_branch, w_out,
                          w_up, f_conv_w, f_conv_b, w_down)
        mod = ada_mod(c_pad, w_ada[l], b_ada[l])[:batch].reshape(batch, 6, 1, d)
        sh_m, sc_m, gt_m, sh_f, sc_f, gt_f = [mod[:, i] for i in range(6)]
        g = norm_g[l].reshape(4, 1, d)
        proj, h = norm_proj(xf, g[0], sc_m, sh_m, p["w_cat"], seq)
        y_ssd = ssd_branch(proj, m_conv_w[l], m_conv_b[l], m_head[l], m_norm_g[l], batch, seq)
        y_rwkv = rwkv_branch(proj, p["mu_rkv"], p["mu_tail"], p["w2p"], p["a2p"], p["g2p"],
                             rwkv_vec[l], batch, seq)
        y_ret = ret_branch(proj, pos_col, batch, seq)
        merged = merge_branches(h, (y_ssd, y_rwkv, y_ret), p["wgate"], p["wbranch"], seq)
        xf = proj_residual(merged, p["wout"], xf, g[1], gt_m, seq)
        xf = conv_geglu_residual(xf, g[2], sc_f, sh_f, p["wg"], p["wu"], p["cw"], p["cb"], p["wd"],
                                 g[3], gt_f, seq)
    return xf.reshape(batch, seq, d)
```

```python
import functools

import numpy as np
import jax
import jax.numpy as jnp
from jax import lax
from jax.experimental import pallas as pl
from jax.experimental.pallas import tpu as pltpu

F32 = jnp.float32
BF16 = jnp.bfloat16
HIGHEST = lax.Precision.HIGHEST

D_MODEL = 2048
HEAD_DIM = 64
BRANCH_DIM = D_MODEL // 2
SSD_CHUNK = 128
M_HEADS = 16
M_GROUPS = 4
M_STATE = 128
M_CONV = 4
M_NORM_EPS = 1e-5
RW_CHUNK = 64
RW_LN_EPS = 64e-5
LORA_W, LORA_A, LORA_G = 96, 96, 256
RET_HEADS = 8
RET_DK = 64
RET_EPS = 1e-6
ROPE_BASE = 10000.0
FFN_DIM = 5504
FFN_PAD = 5632
FFN_TILE = 512
NORM_EPS = 1e-6
LANES = 128
SUBLANES = 8
TAIL = 512
VMEM_LIMIT = 52 * 1024 * 1024

COL_Z, COL_XS, COL_BC = 0, 1024, 2048
COL_RKV = 3072
COL_RQ, COL_RK, COL_RV, COL_RG = 6144, 6656, 7168, 8192
COL_TAIL = 9216
PROJ_COLS = COL_TAIL + TAIL
TAIL_WD, TAIL_AD, TAIL_GD, TAIL_DT = 0, 96, 192, 448


def _cparams(n_axes):
    return pltpu.CompilerParams(dimension_semantics=("arbitrary",) * n_axes,
                                vmem_limit_bytes=VMEM_LIMIT)


def _bdot(a, b):
    return jnp.dot(a.astype(BF16), b.astype(BF16), preferred_element_type=F32)


def _bdot_nt(a, b):
    return lax.dot_general(a.astype(BF16), b.astype(BF16), (((1,), (1,)), ((), ())),
                           preferred_element_type=F32)


def _bdot_tn(a, b):
    return lax.dot_general(a.astype(BF16), b.astype(BF16), (((0,), (0,)), ((), ())),
                           preferred_element_type=F32)


def _hdot(a, b):
    return jnp.dot(a, b, preferred_element_type=F32, precision=HIGHEST)


def _hdot_nt(a, b):
    return lax.dot_general(a, b, (((1,), (1,)), ((), ())), preferred_element_type=F32,
                           precision=HIGHEST)


def _split3(t):
    hi = t.astype(BF16)
    rest = t - hi.astype(F32)
    mid = rest.astype(BF16)
    lo = (rest - mid.astype(F32)).astype(BF16)
    return hi, mid, lo


def _silu(t):
    return t * jax.nn.sigmoid(t)


def _softplus(t):
    return jnp.maximum(t, 0.0) + jnp.log1p(jnp.exp(-jnp.abs(t)))


def _rms(t, eps):
    return t * lax.rsqrt(jnp.mean(t * t, axis=-1, keepdims=True) + eps)


def _shift_rows(cur, prev, k):
    rolled = pltpu.roll(cur, k, 0)
    prev8 = pltpu.roll(prev[prev.shape[0] - SUBLANES:], k, 0)
    row = lax.broadcasted_iota(jnp.int32, (SUBLANES, cur.shape[1]), 0)
    top = jnp.where(row < k, prev8, rolled[:SUBLANES])
    return jnp.concatenate([top, rolled[SUBLANES:]], axis=0)


def _ada_kernel(c_ref, w_ref, b_ref, o_ref):
    o_ref[...] = _bdot(_silu(c_ref[...]), w_ref[0]) + b_ref[0]


def ada_mod(c_pad, w_all, b_all, layer):
    rows, d = c_pad.shape
    depth, _, cols = w_all.shape
    tn = 1024
    return pl.pallas_call(
        _ada_kernel,
        grid=(cols // tn,),
        in_specs=[pl.BlockSpec((rows, d), lambda j: (0, 0)),
                  pl.BlockSpec((1, d, tn), lambda j: (layer, 0, j)),
                  pl.BlockSpec((1, 1, tn), lambda j: (layer, 0, j))],
        out_specs=pl.BlockSpec((rows, tn), lambda j: (0, j)),
        out_shape=jax.ShapeDtypeStruct((rows, cols), F32),
        compiler_params=_cparams(1),
        name="ada_mod",
    )(c_pad, w_all, b_all.reshape(depth, 1, cols))


def _modulated_norm(x, g, sc, sh):
    return _rms(x, NORM_EPS) * g * (1.0 + sc) + sh


def _norm_kernel(x_ref, g_ref, sc_ref, sh_ref, h_ref):
    h_ref[...] = _modulated_norm(x_ref[...], g_ref[...], sc_ref[0], sh_ref[0]).astype(BF16)


def _matmul_kernel(a_ref, w_ref, o_ref):
    o_ref[...] = jnp.dot(a_ref[...], w_ref[...], preferred_element_type=F32)


def norm_proj(x, g, sc, sh, w, seq):
    n, d = x.shape
    cols = w.shape[1]
    tm = min(1024, seq)
    tn = cols // 4
    per_batch = seq // tm
    h = pl.pallas_call(
        _norm_kernel,
        grid=(n // tm,),
        in_specs=[pl.BlockSpec((tm, d), lambda i: (i, 0)),
                  pl.BlockSpec((1, d), lambda i: (0, 0)),
                  pl.BlockSpec((1, 1, d), lambda i: (i // per_batch, 0, 0)),
                  pl.BlockSpec((1, 1, d), lambda i: (i // per_batch, 0, 0))],
        out_specs=pl.BlockSpec((tm, d), lambda i: (i, 0)),
        out_shape=jax.ShapeDtypeStruct((n, d), BF16),
        compiler_params=_cparams(1),
        name="mod_norm",
    )(x, g, sc, sh)
    proj = pl.pallas_call(
        _matmul_kernel,
        grid=(cols // tn, n // tm),
        in_specs=[pl.BlockSpec((tm, d), lambda j, i: (i, 0)),
                  pl.BlockSpec((d, tn), lambda j, i: (0, j))],
        out_specs=pl.BlockSpec((tm, tn), lambda j, i: (i, j)),
        out_shape=jax.ShapeDtypeStruct((n, cols), F32),
        compiler_params=_cparams(2),
        name="in_proj",
    )(h, w)
    return proj, h


def _ssd_kernel(z_ref, xs_ref, bc_ref, tail_ref, cw_ref, cb_ref, dtb_ref, alog_ref, dskip_ref,
                ng_ref, expand_ref, o_ref, prev_s, state_s):
    L = SSD_CHUNK

    nb = z_ref.shape[0]
    n_pairs = M_HEADS // 2
    units = [(b, p) for b in range(nb) for p in range(n_pairs)]

    @pl.when(pl.program_id(0) == 0)
    def _():
        prev_s[...] = jnp.zeros_like(prev_s)
        state_s[...] = jnp.zeros_like(state_s)

    ri = lax.broadcasted_iota(jnp.int32, (L, L), 0)
    ci = lax.broadcasted_iota(jnp.int32, (L, L), 1)
    causal = ri >= ci
    tri = causal.astype(F32)
    eye_h = (lax.broadcasted_iota(jnp.int32, (M_HEADS, M_HEADS), 0)
             == lax.broadcasted_iota(jnp.int32, (M_HEADS, M_HEADS), 1)).astype(F32)
    expand = expand_ref[...]
    cw = cw_ref[...]

    xs, bm, cm, cum, cum_t, xdt, ecum, to_end, elast = [], [], [], [], [], [], [], [], []
    for b in range(nb):
        xbc = jnp.concatenate([xs_ref[b], bc_ref[b]], axis=1)
        prev = prev_s[b]
        prev_s[b] = xbc
        acc = cw[M_CONV - 1:M_CONV] * xbc + cb_ref[...]
        for k in range(1, M_CONV):
            acc = acc + cw[M_CONV - 1 - k:M_CONV - k] * _shift_rows(xbc, prev, k)
        xc = _silu(acc)
        xs.append(xc[:, :BRANCH_DIM])
        bm.append(xc[:, BRANCH_DIM:BRANCH_DIM + M_GROUPS * M_STATE])
        cm.append(xc[:, BRANCH_DIM + M_GROUPS * M_STATE:])
        dt = _softplus(tail_ref[b, :, TAIL_DT:TAIL_DT + M_HEADS] + dtb_ref[...])
        cum.append(_hdot(tri, dt * -jnp.exp(alog_ref[...])))
        cum_t.append(_hdot_nt(eye_h, cum[b]))
        cum_full = _hdot(cum[b], expand)
        cum_last = cum_full[L - 1:L, :]
        xdt.append(xs[b] * _hdot(dt, expand))
        ecum.append(jnp.exp(cum_full))
        to_end.append(jnp.exp(cum_last - cum_full))
        elast.append(jnp.exp(cum_last))

    def group(t, b, g):
        return t[b][:, g * M_STATE:(g + 1) * M_STATE]

    def pair(t, b, p):
        return t[b][:, p * LANES:(p + 1) * LANES]

    first = lax.broadcasted_iota(jnp.int32, (L, LANES), 1) < HEAD_DIM
    cb = {(b, g): _bdot_nt(group(cm, b, g), group(bm, b, g))
          for b in range(nb) for g in range(M_GROUPS)}
    ms, x2 = [], []
    for b, p in units:
        m = [cb[b, p // 2] * jnp.exp(jnp.where(causal, cum[b][:, h:h + 1] - cum_t[b][h:h + 1, :], -1e30))
             for h in (2 * p, 2 * p + 1)]
        ms.append(jnp.concatenate(m, axis=1).astype(BF16))
        xp = pair(xdt, b, p)
        x2.append(jnp.concatenate([jnp.where(first, xp, 0.0), jnp.where(first, 0.0, xp)],
                                  axis=0).astype(BF16))
    y1 = [jnp.dot(ms[i], x2[i], preferred_element_type=F32) for i in range(len(units))]
    st = [state_s[i] for i in range(len(units))]
    y2 = [_bdot(group(cm, b, p // 2), st[i]) for i, (b, p) in enumerate(units)]
    upd = [_bdot_tn(group(bm, b, p // 2), pair(xdt, b, p) * pair(to_end, b, p)) for b, p in units]
    for i, (b, p) in enumerate(units):
        state_s[i] = st[i] * pair(elast, b, p) + upd[i]
    gw = BRANCH_DIM // M_GROUPS
    for b in range(nb):
        y = jnp.concatenate([y1[b * n_pairs + p] + y2[b * n_pairs + p] * pair(ecum, b, p)
                             for p in range(n_pairs)], axis=1)
        y = (y + xs[b] * dskip_ref[...]) * _silu(z_ref[b])
        y = jnp.concatenate([_rms(y[:, g * gw:(g + 1) * gw], M_NORM_EPS) for g in range(M_GROUPS)],
                            axis=1)
        o_ref[b] = (y * ng_ref[...]).astype(BF16)


def ssd_branch(proj, conv_w, conv_b, head_p, norm_g, batch, seq):
    n = proj.shape[0]
    L = SSD_CHUNK
    expand = jnp.repeat(jnp.eye(M_HEADS, dtype=F32), HEAD_DIM, axis=1)
    dskip = jnp.repeat(head_p[2], HEAD_DIM).reshape(1, BRANCH_DIM)
    proj3 = proj.reshape(batch, seq, PROJ_COLS)

    def rows(width, idx):
        return pl.BlockSpec((batch, L, width), lambda c: (0, c, idx))

    def const(shape):
        return pl.BlockSpec(shape, lambda c: (0,) * len(shape))

    out = pl.pallas_call(
        _ssd_kernel,
        grid=(seq // L,),
        in_specs=[rows(1024, COL_Z // 1024), rows(1024, COL_XS // 1024), rows(1024, COL_BC // 1024),
                  rows(TAIL, COL_TAIL // TAIL),
                  const((M_CONV, 2048)), const((1, 2048)), const((1, M_HEADS)), const((1, M_HEADS)),
                  const((1, BRANCH_DIM)), const((1, BRANCH_DIM)), const((M_HEADS, BRANCH_DIM))],
        out_specs=rows(BRANCH_DIM, 0),
        out_shape=jax.ShapeDtypeStruct((batch, seq, BRANCH_DIM), BF16),
        scratch_shapes=[pltpu.VMEM((batch, L, 2048), F32),
                        pltpu.VMEM((batch * M_HEADS // 2, M_STATE, LANES), F32)],
        compiler_params=_cparams(1),
        name="ssd_branch",
    )(proj3, proj3, proj3, proj3, conv_w, conv_b.reshape(1, -1), head_p[0].reshape(1, -1),
      head_p[1].reshape(1, -1), dskip, norm_g.reshape(1, -1), expand)
    return out.reshape(n, BRANCH_DIM)


def _rwkv_kernel(rkv_ref, tail_ref, mu_rkv_ref, mu_tail_ref, w2_ref, a2_ref, g2_ref, vec_ref,
                 o_ref, prev_rkv_s, prev_tail_s, state_s):
    C = RW_CHUNK
    R = 2 * C
    nb = rkv_ref.shape[0]
    n_pairs = BRANCH_DIM // LANES
    units = [(b, p) for b in range(nb) for p in range(n_pairs)]
    n_u = len(units)

    @pl.when(pl.program_id(0) == 0)
    def _():
        prev_rkv_s[...] = jnp.zeros_like(prev_rkv_s)
        prev_tail_s[...] = jnp.zeros_like(prev_tail_s)
        state_s[...] = jnp.zeros_like(state_s)

    def token_shift(ref, prev_s, mu):
        parts = []
        for b in range(nb):
            raw = ref[b]
            parts.append(raw + (_shift_rows(raw, prev_s[b], 1) - raw) * mu)
            prev_s[b] = raw[C - SUBLANES:]
        return jnp.concatenate(parts, axis=0)

    rkv = token_shift(rkv_ref, prev_rkv_s, mu_rkv_ref[...])
    tail = token_shift(tail_ref, prev_tail_s, mu_tail_ref[...])
    r = rkv[:, :BRANCH_DIM]
    k = rkv[:, BRANCH_DIM:2 * BRANCH_DIM]
    v = rkv[:, 2 * BRANCH_DIM:]

    vec = vec_ref[...]
    w0, a0, k_k, k_a, r_k, ln_w, ln_b = [vec[i:i + 1] for i in range(7)]
    log_w = -_softplus(-(w0 + _bdot(jnp.tanh(tail), w2_ref[...]))) - 0.5
    neg_log_decay = jnp.exp(log_w)
    a = jax.nn.sigmoid(a0 + _bdot(tail, a2_ref[...]))
    g = _bdot(jax.nn.sigmoid(tail), g2_ref[...])

    lane = lax.broadcasted_iota(jnp.int32, (R, R), 1)
    sub = lax.broadcasted_iota(jnp.int32, (R, R), 0)
    same_head = (lane >> 6) == (sub >> 6)
    head_ones2 = jnp.concatenate([same_head.astype(BF16)] * 2, axis=0)
    strict = same_head & (sub > lane)
    incl = same_head & (sub >= lane)
    eye = (sub == lane).astype(F32)
    first = lax.broadcasted_iota(jnp.int32, (C, LANES), 1) < HEAD_DIM

    def head_sum(t):
        return jnp.dot(jnp.concatenate(_split3(t)[:2], axis=1), head_ones2, preferred_element_type=F32)

    rows = nb * C
    ri = lax.broadcasted_iota(jnp.int32, (rows, rows), 0)
    ci = lax.broadcasted_iota(jnp.int32, (rows, rows), 1)
    tri3 = jnp.concatenate([(((ri >> 6) == (ci >> 6)) & (ri >= ci)).astype(BF16)] * 3, axis=1)
    cl = -jnp.dot(tri3, jnp.concatenate(_split3(neg_log_decay), axis=0), preferred_element_type=F32)

    kk = k * k_k
    k2 = k * (1.0 + (a - 1.0) * k_a)

    def unit(t, b, p):
        return t[b * C:(b + 1) * C, p * LANES:(p + 1) * LANES]

    def lanes(t, p):
        return t[:, p * LANES:(p + 1) * LANES]

    def stack(t):
        return jnp.concatenate([jnp.where(first, t, 0.0), jnp.where(first, 0.0, t)], axis=0)

    def dup(t):
        return jnp.concatenate([t, t], axis=0)

    kk_u = [unit(kk, b, p) for b, p in units]
    k2_u = [unit(k2, b, p) for b, p in units]
    r_u = [unit(r, b, p) for b, p in units]
    v_u = [unit(v, b, p) for b, p in units]
    cl_u = [unit(cl, b, p) for b, p in units]
    nld_u = [unit(neg_log_decay, b, p) for b, p in units]
    a_u = [unit(a, b, p) for b, p in units]

    sums = head_sum(jnp.concatenate(
        [t * t for t in kk_u] + [r_u[i] * k2_u[i] * lanes(r_k, units[i][1]) for i in range(n_u)],
        axis=0))
    kk_u = [kk_u[i] * lax.rsqrt(jnp.maximum(sums[i * C:(i + 1) * C], 1e-24)) for i in range(n_u)]
    bonus_u = [sums[(n_u + i) * C:(n_u + i + 1) * C] * v_u[i] for i in range(n_u)]
    b_u = [kk_u[i] * a_u[i] for i in range(n_u)]

    w_inv = [jnp.exp(-t) for t in cl_u]
    lhs_u = [jnp.concatenate([stack(kk_u[i] * jnp.exp(cl_u[i] + nld_u[i])),
                              stack(r_u[i] * jnp.exp(cl_u[i]))], axis=0).astype(BF16)
             for i in range(n_u)]
    rhs_u = [jnp.concatenate([dup(b_u[i] * w_inv[i]), dup(k2_u[i] * w_inv[i])], axis=0).astype(BF16)
             for i in range(n_u)]
    gram_u = [lax.dot_general(lhs_u[i], rhs_u[i], (((1,), (1,)), ((), ())),
                              preferred_element_type=F32) for i in range(n_u)]
    a_b = [jnp.where(strict, t[:R, :R], 0.0) for t in gram_u]
    a_k = [jnp.where(strict, t[:R, R:], 0.0).astype(BF16) for t in gram_u]
    l_bk = [jnp.concatenate([jnp.where(incl, t[R:, :R], 0.0), jnp.where(incl, t[R:, R:], 0.0)],
                            axis=1).astype(BF16) for t in gram_u]
    minv = [eye - t for t in a_b]
    pw = [t.astype(BF16) for t in a_b]
    for _ in range(int(np.log2(C)) - 1):
        pw = [jnp.dot(t, t, preferred_element_type=F32).astype(BF16) for t in pw]
        minv = [minv[i] + jnp.dot(minv[i].astype(BF16), pw[i], preferred_element_type=F32)
                for i in range(n_u)]
    minv = [t.astype(BF16) for t in minv]

    vs = [stack(t).astype(BF16) for t in v_u]
    akv = [jnp.dot(a_k[i], vs[i], preferred_element_type=F32) for i in range(n_u)]
    st = [state_s[i] for i in range(n_u)]
    sp = [_bdot_nt(lhs_u[i], st[i]) for i in range(n_u)]
    u = [jnp.dot(minv[i], (-sp[i][:R] - akv[i]).astype(BF16), preferred_element_type=F32)
         for i in range(n_u)]
    uv = [jnp.concatenate([u[i].astype(BF16), vs[i]], axis=0) for i in range(n_u)]
    ysk = [sp[i][R:] + jnp.dot(l_bk[i], uv[i], preferred_element_type=F32) for i in range(n_u)]
    y_all = jnp.concatenate([t[:C] + t[C:] for t in ysk], axis=0)
    for i in range(n_u):
        cl_last = cl_u[i][C - 1:C]
        w_end = jnp.exp(cl_last - cl_u[i])
        ends = jnp.concatenate([stack(b_u[i] * w_end), stack(k2_u[i] * w_end)], axis=0)
        state_s[i] = st[i] * jnp.exp(cl_last) + _bdot_tn(uv[i], ends)

    dev = y_all - head_sum(y_all) * (1.0 / HEAD_DIM)
    var = head_sum(dev * dev) * (1.0 / HEAD_DIM)
    yn = dev * lax.rsqrt(var + RW_LN_EPS)
    for b in range(nb):
        outs = []
        for p in range(n_pairs):
            i = b * n_pairs + p
            y = yn[i * C:(i + 1) * C] * lanes(ln_w, p) + lanes(ln_b, p)
            outs.append((y + bonus_u[i]) * unit(g, b, p))
        o_ref[b] = jnp.concatenate(outs, axis=1).astype(BF16)


def rwkv_branch(proj, mu_rkv, mu_tail, w2p, a2p, g2p, vec, batch, seq):
    n = proj.shape[0]
    C = RW_CHUNK
    proj3 = proj.reshape(batch, seq, PROJ_COLS)

    def rows(width, idx):
        return pl.BlockSpec((batch, C, width), lambda c: (0, c, idx))

    def const(shape):
        return pl.BlockSpec(shape, lambda c: (0,) * len(shape))

    out = pl.pallas_call(
        _rwkv_kernel,
        grid=(seq // C,),
        in_specs=[rows(3 * BRANCH_DIM, COL_RKV // (3 * BRANCH_DIM)), rows(TAIL, COL_TAIL // TAIL),
                  const((1, 3 * BRANCH_DIM)), const((1, TAIL)),
                  const((TAIL, BRANCH_DIM)), const((TAIL, BRANCH_DIM)), const((TAIL, BRANCH_DIM)),
                  const((7, BRANCH_DIM))],
        out_specs=rows(BRANCH_DIM, 0),
        out_shape=jax.ShapeDtypeStruct((batch, seq, BRANCH_DIM), BF16),
        scratch_shapes=[pltpu.VMEM((batch, SUBLANES, 3 * BRANCH_DIM), F32),
                        pltpu.VMEM((batch, SUBLANES, TAIL), F32),
                        pltpu.VMEM((batch * BRANCH_DIM // LANES, LANES, LANES), F32)],
        compiler_params=_cparams(1),
        name="rwkv_branch",
    )(proj3, proj3, mu_rkv, mu_tail, w2p, a2p, g2p, vec)
    return out.reshape(n, BRANCH_DIM)


def _ret_kernel(q_ref, k_ref, v_ref, g_ref, pos_ref, invf_ref, dmask_ref, qdec_ref, kdec_ref,
                cdec_ref, o_ref, state_s):
    L = SSD_CHUNK
    half = RET_DK // 2
    width = RET_HEADS * RET_DK
    nb = q_ref.shape[0]
    units = [(b, h) for b in range(nb) for h in range(RET_HEADS)]
    n_u = len(units)

    @pl.when(pl.program_id(0) == 0)
    def _():
        state_s[...] = jnp.zeros_like(state_s)

    lane_w = lax.broadcasted_iota(jnp.int32, (L, width), 1)
    low_half = (lane_w & (RET_DK - 1)) < half
    q, k, qd, kd = [], [], [], []
    for b in range(nb):
        ang = pos_ref[b].astype(F32) * invf_ref[...]
        cos = jnp.concatenate([jnp.cos(ang)] * (width // LANES), axis=1)
        sin = jnp.concatenate([jnp.sin(ang)] * (width // LANES), axis=1)

        def rope(t):
            rot = jnp.where(low_half, -pltpu.roll(t, width - half, 1), pltpu.roll(t, half, 1))
            return t * cos + rot * sin

        q.append(rope(q_ref[b]))
        k.append(rope(k_ref[b]) * (RET_DK ** -0.5))
        qd.append(q[b] * qdec_ref[...])
        kd.append(k[b] * kdec_ref[...])

    lane = lax.broadcasted_iota(jnp.int32, (L, LANES), 1)

    def head(t, b, h, masked=True):
        t = t[b][:, (h // 2) * LANES:(h // 2 + 1) * LANES]
        return jnp.where((lane >> 6) == (h % 2), t, 0.0) if masked else t

    v_u = [v_ref[b, :, h * LANES:(h + 1) * LANES].astype(BF16) for b, h in units]
    scores = [_bdot_nt(head(q, b, h), head(k, b, h, masked=False)) * dmask_ref[h] for b, h in units]
    st = [state_s[i] for i in range(n_u)]
    y = [jnp.dot(scores[i].astype(BF16), v_u[i], preferred_element_type=F32) for i in range(n_u)]
    y2 = [_bdot(head(qd, b, h), st[i]) for i, (b, h) in enumerate(units)]
    upd = [_bdot_tn(head(kd, b, h), v_u[i]) for i, (b, h) in enumerate(units)]
    for i, (b, h) in enumerate(units):
        state_s[i] = st[i] * cdec_ref[h] + upd[i]
    for b in range(nb):
        outs = [_rms(y[b * RET_HEADS + h] + y2[b * RET_HEADS + h], RET_EPS) for h in range(RET_HEADS)]
        o_ref[b] = (_silu(g_ref[b]) * jnp.concatenate(outs, axis=1)).astype(BF16)


def ret_branch(proj, pos_col, batch, seq):
    n = proj.shape[0]
    L = SSD_CHUNK
    nc = seq // L
    half = RET_DK // 2
    inv_freq = ROPE_BASE ** (-jnp.arange(half, dtype=F32) / half)
    invf = jnp.tile(inv_freq, LANES // half).reshape(1, LANES)
    log_gamma = jnp.log1p(-jnp.exp2(-5.0 - jnp.arange(RET_HEADS, dtype=F32)))
    idx = jnp.arange(L, dtype=F32)
    rel = (idx[:, None] - idx[None, :])[None]
    dmask = jnp.exp(jnp.where(rel >= 0, rel * log_gamma[:, None, None], -jnp.inf))
    qdec = jnp.repeat(jnp.exp((idx[:, None] + 1.0) * log_gamma), RET_DK, axis=1)
    kdec = jnp.repeat(jnp.exp((L - 1.0 - idx)[:, None] * log_gamma), RET_DK, axis=1)
    cdec = jnp.broadcast_to(jnp.exp(L * log_gamma)[:, None, None], (RET_HEADS, 1, LANES))
    proj3 = proj.reshape(batch, seq, PROJ_COLS)

    def rows(width, idx_):
        return pl.BlockSpec((batch, L, width), lambda c: (0, c, idx_))

    def const(shape):
        return pl.BlockSpec(shape, lambda c: (0,) * len(shape))

    out = pl.pallas_call(
        _ret_kernel,
        grid=(seq // L,),
        in_specs=[rows(512, COL_RQ // 512), rows(512, COL_RK // 512), rows(1024, COL_RV // 1024),
                  rows(1024, COL_RG // 1024), rows(1, 0),
                  const((1, LANES)), const((RET_HEADS, L, L)), const((L, 512)), const((L, 512)),
                  const((RET_HEADS, 1, LANES))],
        out_specs=rows(BRANCH_DIM, 0),
        out_shape=jax.ShapeDtypeStruct((batch, seq, BRANCH_DIM), BF16),
        scratch_shapes=[pltpu.VMEM((batch * RET_HEADS, LANES, LANES), F32)],
        compiler_params=_cparams(1),
        name="ret_branch",
    )(proj3, proj3, proj3, proj3, pos_col.reshape(batch, seq, 1), invf, dmask, qdec, kdec, cdec)
    return out.reshape(n, BRANCH_DIM)


def _merge_kernel(h_ref, y0_ref, y1_ref, y2_ref, wg_ref, wb_ref, o_ref):
    h = h_ref[...]
    acc = None
    for b, y_ref in enumerate((y0_ref, y1_ref, y2_ref)):
        gate = jax.nn.sigmoid(jnp.dot(h, wg_ref[b], preferred_element_type=F32))
        term = gate * jnp.dot(y_ref[...], wb_ref[b], preferred_element_type=F32)
        acc = term if acc is None else acc + term
    o_ref[...] = acc.astype(BF16)


def merge_branches(h, ys, wg, wb, seq):
    n, d = h.shape
    tm = min(1024, seq)
    tn = 512
    yspec = pl.BlockSpec((tm, BRANCH_DIM), lambda i, j: (i, 0))
    return pl.pallas_call(
        _merge_kernel,
        grid=(n // tm, d // tn),
        in_specs=[pl.BlockSpec((tm, d), lambda i, j: (i, 0)), yspec, yspec, yspec,
                  pl.BlockSpec((3, d, tn), lambda i, j: (0, 0, j)),
                  pl.BlockSpec((3, BRANCH_DIM, tn), lambda i, j: (0, 0, j))],
        out_specs=pl.BlockSpec((tm, tn), lambda i, j: (i, j)),
        out_shape=jax.ShapeDtypeStruct((n, d), BF16),
        compiler_params=_cparams(2),
        name="merge_branches",
    )(h, *ys, wg, wb)


def _proj_residual_kernel(a_ref, w_ref, x_ref, g_ref, gt_ref, o_ref):
    y = jnp.dot(a_ref[...], w_ref[...], preferred_element_type=F32)
    o_ref[...] = x_ref[...] + gt_ref[0] * (_rms(y, NORM_EPS) * g_ref[...])


def proj_residual(a, w, x, g, gt, seq):
    n, d = x.shape
    tm = min(512, seq)
    per_batch = seq // tm
    return pl.pallas_call(
        _proj_residual_kernel,
        grid=(n // tm,),
        in_specs=[pl.BlockSpec((tm, d), lambda i: (i, 0)),
                  pl.BlockSpec((d, d), lambda i: (0, 0)),
                  pl.BlockSpec((tm, d), lambda i: (i, 0)),
                  pl.BlockSpec((1, d), lambda i: (0, 0)),
                  pl.BlockSpec((1, 1, d), lambda i: (i // per_batch, 0, 0))],
        out_specs=pl.BlockSpec((tm, d), lambda i: (i, 0)),
        out_shape=jax.ShapeDtypeStruct((n, d), F32),
        compiler_params=_cparams(1),
        name="proj_residual",
    )(a, w, x, g, gt)


def _gelu_tanh(t):
    return 0.5 * t * (1.0 + jnp.tanh(np.sqrt(2.0 / np.pi).astype(np.float32)
                                     * (t + 0.044715 * (t * t * t))))


def _ffn_kernel(x_ref, g2_ref, sc_ref, sh_ref, wg_ref, wu_ref, cw_ref, cb_ref, wd_ref, g3_ref,
                gt_ref, o_ref, h_s, acc_s, carry_s, pre_even_s, pre_odd_s, act_s, *, per_batch, nj):
    i = pl.program_id(0)
    s = pl.program_id(1)
    tm = x_ref.shape[0]
    seq_start = (i % per_batch) == 0

    n_chunks = 2
    n_pieces = 4
    rc = tm // n_chunks

    tf = wg_ref.shape[1]

    def up_proj_piece(pre_s, c):
        half, cols = c // 2, slice((c % 2) * (tf // 2), (c % 2 + 1) * (tf // 2))
        w_ref = wu_ref if half else wg_ref
        pre_s[half, :, cols] = jnp.dot(h_s[...], w_ref[:, cols], preferred_element_type=F32)

    def up_proj(pre_s):
        for c in range(n_pieces):
            up_proj_piece(pre_s, c)

    def conv(pre_s, half, c):
        off = pl.multiple_of((s - 1) * SUBLANES, SUBLANES)
        pre = pre_s[half, c * rc:(c + 1) * rc, :]
        if c == 0:
            prev = jnp.where(seq_start, 0.0, carry_s[half, pl.ds(off, SUBLANES), :])
        else:
            prev = pre_s[half, c * rc - SUBLANES:c * rc, :]
        if c == n_chunks - 1:
            carry_s[half, pl.ds(off, SUBLANES), :] = pre[rc - SUBLANES:]
        cw = cw_ref[half]
        return (cw[0:1] * _shift_rows(pre, prev, 2) + cw[1:2] * _shift_rows(pre, prev, 1)
                + cw[2:3] * pre + cb_ref[half])

    def down_proj(pre_s, pre_next_s=None):
        for c in range(n_chunks):
            rows = slice(c * rc, (c + 1) * rc)
            act_s[rows, :] = (_gelu_tanh(conv(pre_s, 0, c)) * conv(pre_s, 1, c)).astype(BF16)
            if pre_next_s is not None:
                for piece in range(c * n_pieces // n_chunks, (c + 1) * n_pieces // n_chunks):
                    up_proj_piece(pre_next_s, piece)
            acc_s[rows, :] += jnp.dot(act_s[rows, :], wd_ref[...], preferred_element_type=F32)

    @pl.when(s == 0)
    def _():
        h_s[...] = _modulated_norm(x_ref[...], g2_ref[...], sc_ref[0], sh_ref[0]).astype(BF16)
        acc_s[...] = jnp.zeros_like(acc_s)
        up_proj(pre_even_s)

    @pl.when((s > 0) & (s < nj) & (s % 2 == 1))
    def _():
        down_proj(pre_even_s, pre_odd_s)

    @pl.when((s > 0) & (s < nj) & (s % 2 == 0))
    def _():
        down_proj(pre_odd_s, pre_even_s)

    @pl.when(s == nj)
    def _():
        down_proj(pre_even_s if (nj - 1) % 2 == 0 else pre_odd_s)
        o_ref[...] = x_ref[...] + gt_ref[0] * (_rms(acc_s[...], NORM_EPS) * g3_ref[...])


def conv_geglu_residual(x, g2, sc, sh, wg, wu, cw, cb, wd, g3, gt, seq):
    n, d = x.shape
    fp = wg.shape[1]
    tm = min(512, seq)
    tf = FFN_TILE
    nj = fp // tf
    per_batch = seq // tm
    mod = lambda i, s: (i // per_batch, 0, 0)
    ahead = lambda i, s: (0, jnp.minimum(s, nj - 1))
    behind = lambda i, s: (0, 0, jnp.maximum(s - 1, 0))
    return pl.pallas_call(
        functools.partial(_ffn_kernel, per_batch=per_batch, nj=nj),
        grid=(n // tm, nj + 1),
        in_specs=[pl.BlockSpec((tm, d), lambda i, s: (i, 0)),
                  pl.BlockSpec((1, d), lambda i, s: (0, 0)),
                  pl.BlockSpec((1, 1, d), mod), pl.BlockSpec((1, 1, d), mod),
                  pl.BlockSpec((d, tf), ahead),
                  pl.BlockSpec((d, tf), ahead),
                  pl.BlockSpec((2, 3, tf), behind),
                  pl.BlockSpec((2, 1, tf), behind),
                  pl.BlockSpec((tf, d), lambda i, s: (jnp.maximum(s - 1, 0), 0)),
                  pl.BlockSpec((1, d), lambda i, s: (0, 0)),
                  pl.BlockSpec((1, 1, d), mod)],
        out_specs=pl.BlockSpec((tm, d), lambda i, s: (i, 0)),
        out_shape=jax.ShapeDtypeStruct((n, d), F32),
        scratch_shapes=[pltpu.VMEM((tm, d), BF16), pltpu.VMEM((tm, d), F32),
                        pltpu.VMEM((2, nj * SUBLANES, tf), F32),
                        pltpu.VMEM((2, tm, tf), F32), pltpu.VMEM((2, tm, tf), F32),
                        pltpu.VMEM((tm, tf), BF16)],
        compiler_params=_cparams(2),
        name="conv_geglu",
    )(x, g2, sc, sh, wg, wu, cw, cb, wd, g3, gt)


def _layer_params(l, w_in, rwkv_mu, rwkv_w2, rwkv_a2, rwkv_g2, w_gate, w_branch, w_out, w_up,
                  f_conv_w, f_conv_b, w_down):
    m_cols = BRANCH_DIM + 2048 + M_HEADS
    rw_cols = 3 * BRANCH_DIM + LORA_W + LORA_A + LORA_G
    wm, wr, wt = w_in[l, :, :m_cols], w_in[l, :, m_cols:m_cols + rw_cols], w_in[l, :, m_cols + rw_cols:]
    n_lora = LORA_W + LORA_A + LORA_G
    w_cat = jnp.concatenate([
        wm[:, :BRANCH_DIM + 2048], wr[:, :3 * BRANCH_DIM], wt,
        wr[:, 3 * BRANCH_DIM:], wm[:, BRANCH_DIM + 2048:],
        jnp.zeros((D_MODEL, TAIL - n_lora - M_HEADS), F32)], axis=1).astype(BF16)
    mu = rwkv_mu[l]
    mu_rkv = mu[:3 * BRANCH_DIM].reshape(1, -1)
    mu_tail = jnp.concatenate([mu[3 * BRANCH_DIM:], jnp.zeros((TAIL - n_lora,), F32)]).reshape(1, -1)

    def pad_rows(w, start):
        return jnp.zeros((TAIL, BRANCH_DIM), F32).at[start:start + w.shape[0]].set(w).astype(BF16)

    fpad = FFN_PAD - FFN_DIM
    wg = jnp.pad(w_up[l, :, :FFN_DIM], ((0, 0), (0, fpad))).astype(BF16)
    wu = jnp.pad(w_up[l, :, FFN_DIM:], ((0, 0), (0, fpad))).astype(BF16)
    cw = jnp.pad(jnp.stack([f_conv_w[l, :, :FFN_DIM], f_conv_w[l, :, FFN_DIM:]]),
                 ((0, 0), (0, 0), (0, fpad)))
    cb = jnp.pad(jnp.stack([f_conv_b[l, :FFN_DIM], f_conv_b[l, FFN_DIM:]]),
                 ((0, 0), (0, fpad)))[:, None, :]
    wd = jnp.pad(w_down[l], ((0, fpad), (0, 0))).astype(BF16)
    return dict(w_cat=w_cat, mu_rkv=mu_rkv, mu_tail=mu_tail,
                w2p=pad_rows(rwkv_w2[l], TAIL_WD), a2p=pad_rows(rwkv_a2[l], TAIL_AD),
                g2p=pad_rows(rwkv_g2[l], TAIL_GD),
                wgate=w_gate[l].astype(BF16), wbranch=w_branch[l].astype(BF16),
                wout=w_out[l].astype(BF16), wg=wg, wu=wu, cw=cw, cb=cb, wd=wd)


def kernel(x, c, positions, w_ada, b_ada, norm_g, w_in, m_conv_w, m_conv_b, m_head, m_norm_g,
           rwkv_mu, rwkv_w2, rwkv_a2, rwkv_g2, rwkv_vec, w_gate, w_branch, w_out, w_up, f_conv_w,
           f_conv_b, w_down):
    batch, seq, d = x.shape
    depth = w_ada.shape[0]
    n = batch * seq
    xf = x.reshape(n, d)
    pos_col = positions.reshape(n, 1)
    c_pad = jnp.zeros((SUBLANES, d), F32).at[:batch].set(c)
    for l in range(depth):
        p = _layer_params(l, w_in, rwkv_mu, rwkv_w2, rwkv_a2, rwkv_g2, w_gate, w_branch, w_out,
                          w_up, f_conv_w, f_conv_b, w_down)
        mod = ada_mod(c_pad, w_ada, b_ada, l)[:batch].reshape(batch, 6, 1, d)
        sh_m, sc_m, gt_m, sh_f, sc_f, gt_f = [mod[:, i] for i in range(6)]
        g = norm_g[l].reshape(4, 1, d)
        proj, h = norm_proj(xf, g[0], sc_m, sh_m, p["w_cat"], seq)
        y_ssd = ssd_branch(proj, m_conv_w[l], m_conv_b[l], m_head[l], m_norm_g[l], batch, seq)
        y_rwkv = rwkv_branch(proj, p["mu_rkv"], p["mu_tail"], p["w2p"], p["a2p"], p["g2p"],
                             rwkv_vec[l], batch, seq)
        y_ret = ret_branch(proj, pos_col, batch, seq)
        merged = merge_branches(h, (y_ssd, y_rwkv, y_ret), p["wgate"], p["wbranch"], seq)
        xf = proj_residual(merged, p["wout"], xf, g[1], gt_m, seq)
        xf = conv_geglu_residual(xf, g[2], sc_f, sh_f, p["wg"], p["wu"], p["cw"], p["cb"], p["wd"],
                                 g[3], gt_f, seq)
    return xf.reshape(batch, seq, d)
```

```python
import functools

import numpy as np
import jax
import jax.numpy as jnp
from jax import lax
from jax.experimental import pallas as pl
from jax.experimental.pallas import tpu as pltpu

F32 = jnp.float32
BF16 = jnp.bfloat16
HIGHEST = lax.Precision.HIGHEST

D_MODEL = 2048
HEAD_DIM = 64
BRANCH_DIM = D_MODEL // 2
SSD_CHUNK = 128
M_HEADS = 16
M_GROUPS = 4
M_STATE = 128
M_CONV = 4
M_NORM_EPS = 1e-5
RW_CHUNK = 64
RW_LN_EPS = 64e-5
LORA_W, LORA_A, LORA_G = 96, 96, 256
RET_HEADS = 8
RET_DK = 64
RET_EPS = 1e-6
ROPE_BASE = 10000.0
FFN_DIM = 5504
FFN_PAD = 5632
FFN_TILE = 512
NORM_EPS = 1e-6
LANES = 128
SUBLANES = 8
TAIL = 512
VMEM_LIMIT = 52 * 1024 * 1024

COL_Z, COL_XS, COL_BC = 0, 1024, 2048
COL_RKV = 3072
COL_RQ, COL_RK, COL_RV, COL_RG = 6144, 6656, 7168, 8192
COL_TAIL = 9216
PROJ_COLS = COL_TAIL + TAIL
TAIL_WD, TAIL_AD, TAIL_GD, TAIL_DT = 0, 96, 192, 448


def _cparams(n_axes):
    return pltpu.CompilerParams(dimension_semantics=("arbitrary",) * n_axes,
                                vmem_limit_bytes=VMEM_LIMIT)


def _bdot(a, b):
    return jnp.dot(a.astype(BF16), b.astype(BF16), preferred_element_type=F32)


def _bdot_nt(a, b):
    return lax.dot_general(a.astype(BF16), b.astype(BF16), (((1,), (1,)), ((), ())),
                           preferred_element_type=F32)


def _bdot_tn(a, b):
    return lax.dot_general(a.astype(BF16), b.astype(BF16), (((0,), (0,)), ((), ())),
                           preferred_element_type=F32)


def _hdot(a, b):
    return jnp.dot(a, b, preferred_element_type=F32, precision=HIGHEST)


def _hdot_nt(a, b):
    return lax.dot_general(a, b, (((1,), (1,)), ((), ())), preferred_element_type=F32,
                           precision=HIGHEST)


def _split3(t):
    hi = t.astype(BF16)
    rest = t - hi.astype(F32)
    mid = rest.astype(BF16)
    lo = (rest - mid.astype(F32)).astype(BF16)
    return hi, mid, lo


def _silu(t):
    return t * jax.nn.sigmoid(t)


def _softplus(t):
    return jnp.maximum(t, 0.0) + jnp.log1p(jnp.exp(-jnp.abs(t)))


def _rms(t, eps):
    return t * lax.rsqrt(jnp.mean(t * t, axis=-1, keepdims=True) + eps)


def _zero_at_first_step(*scratch):
    @pl.when(pl.program_id(0) == 0)
    def _():
        for ref in scratch:
            ref[...] = jnp.zeros_like(ref)


def _shift_rows(cur, prev, k):
    rolled = pltpu.roll(cur, k, 0)
    prev8 = pltpu.roll(prev[prev.shape[0] - SUBLANES:], k, 0)
    row = lax.broadcasted_iota(jnp.int32, (SUBLANES, cur.shape[1]), 0)
    top = jnp.where(row < k, prev8, rolled[:SUBLANES])
    return jnp.concatenate([top, rolled[SUBLANES:]], axis=0)


def _ada_kernel(c_ref, w_ref, b_ref, o_ref):
    o_ref[...] = _bdot(_silu(c_ref[...]), w_ref[0]) + b_ref[0]


def ada_mod(c_pad, w_all, b_all, layer):
    rows, d = c_pad.shape
    depth, _, cols = w_all.shape
    tn = 1024
    return pl.pallas_call(
        _ada_kernel,
        grid=(cols // tn,),
        in_specs=[pl.BlockSpec((rows, d), lambda j: (0, 0)),
                  pl.BlockSpec((1, d, tn), lambda j: (layer, 0, j)),
                  pl.BlockSpec((1, 1, tn), lambda j: (layer, 0, j))],
        out_specs=pl.BlockSpec((rows, tn), lambda j: (0, j)),
        out_shape=jax.ShapeDtypeStruct((rows, cols), F32),
        compiler_params=_cparams(1),
        name="ada_mod",
    )(c_pad, w_all, b_all.reshape(depth, 1, cols))


def _modulated_norm(x, g, sc, sh):
    return _rms(x, NORM_EPS) * g * (1.0 + sc) + sh


def _norm_kernel(x_ref, g_ref, sc_ref, sh_ref, h_ref):
    h_ref[...] = _modulated_norm(x_ref[...], g_ref[...], sc_ref[0], sh_ref[0]).astype(BF16)


def _matmul_kernel(a_ref, w_ref, o_ref):
    o_ref[...] = jnp.dot(a_ref[...], w_ref[...], preferred_element_type=F32)


def norm_proj(x, g, sc, sh, w, seq):
    n, d = x.shape
    cols = w.shape[1]
    tm = min(1024, seq)
    tn = cols // 4
    per_batch = seq // tm
    h = pl.pallas_call(
        _norm_kernel,
        grid=(n // tm,),
        in_specs=[pl.BlockSpec((tm, d), lambda i: (i, 0)),
                  pl.BlockSpec((1, d), lambda i: (0, 0)),
                  pl.BlockSpec((1, 1, d), lambda i: (i // per_batch, 0, 0)),
                  pl.BlockSpec((1, 1, d), lambda i: (i // per_batch, 0, 0))],
        out_specs=pl.BlockSpec((tm, d), lambda i: (i, 0)),
        out_shape=jax.ShapeDtypeStruct((n, d), BF16),
        compiler_params=_cparams(1),
        name="mod_norm",
    )(x, g, sc, sh)
    proj = pl.pallas_call(
        _matmul_kernel,
        grid=(cols // tn, n // tm),
        in_specs=[pl.BlockSpec((tm, d), lambda j, i: (i, 0)),
                  pl.BlockSpec((d, tn), lambda j, i: (0, j))],
        out_specs=pl.BlockSpec((tm, tn), lambda j, i: (i, j)),
        out_shape=jax.ShapeDtypeStruct((n, cols), F32),
        compiler_params=_cparams(2),
        name="in_proj",
    )(h, w)
    return proj, h


def _ssd_kernel(z_ref, xs_ref, bc_ref, tail_ref, cw_ref, cb_ref, dtb_ref, alog_ref, dskip_ref,
                ng_ref, expand_ref, o_ref, prev_s, state_s, *, seq=0, init=True):
    L = SSD_CHUNK

    nb = z_ref.shape[0]
    n_pairs = M_HEADS // 2
    units = [(b, p) for b in range(nb) for p in range(n_pairs)]

    if init:
        _zero_at_first_step(prev_s, state_s)

    ri = lax.broadcasted_iota(jnp.int32, (L, L), 0)
    ci = lax.broadcasted_iota(jnp.int32, (L, L), 1)
    causal = ri >= ci
    tri = causal.astype(F32)
    eye_h = (lax.broadcasted_iota(jnp.int32, (M_HEADS, M_HEADS), 0)
             == lax.broadcasted_iota(jnp.int32, (M_HEADS, M_HEADS), 1)).astype(F32)
    expand = expand_ref[...]
    cw = cw_ref[...]

    xs, bm, cm, cum, cum_t, xdt, ecum, to_end, elast = [], [], [], [], [], [], [], [], []
    for b in range(nb):
        xbc = jnp.concatenate([xs_ref[b], bc_ref[b]], axis=1)
        prev = prev_s[seq + b]
        prev_s[seq + b] = xbc
        acc = cw[M_CONV - 1:M_CONV] * xbc + cb_ref[...]
        for k in range(1, M_CONV):
            acc = acc + cw[M_CONV - 1 - k:M_CONV - k] * _shift_rows(xbc, prev, k)
        xc = _silu(acc)
        xs.append(xc[:, :BRANCH_DIM])
        bm.append(xc[:, BRANCH_DIM:BRANCH_DIM + M_GROUPS * M_STATE])
        cm.append(xc[:, BRANCH_DIM + M_GROUPS * M_STATE:])
        dt = _softplus(tail_ref[b, :, TAIL_DT:TAIL_DT + M_HEADS] + dtb_ref[...])
        cum.append(_hdot(tri, dt * -jnp.exp(alog_ref[...])))
        cum_t.append(_hdot_nt(eye_h, cum[b]))
        cum_full = _hdot(cum[b], expand)
        cum_last = cum_full[L - 1:L, :]
        xdt.append(xs[b] * _hdot(dt, expand))
        ecum.append(jnp.exp(cum_full))
        to_end.append(jnp.exp(cum_last - cum_full))
        elast.append(jnp.exp(cum_last))

    def group(t, b, g):
        return t[b][:, g * M_STATE:(g + 1) * M_STATE]

    def pair(t, b, p):
        return t[b][:, p * LANES:(p + 1) * LANES]

    first = lax.broadcasted_iota(jnp.int32, (L, LANES), 1) < HEAD_DIM
    cb = {(b, g): _bdot_nt(group(cm, b, g), group(bm, b, g))
          for b in range(nb) for g in range(M_GROUPS)}
    ms, x2 = [], []
    for b, p in units:
        m = [cb[b, p // 2] * jnp.exp(jnp.where(causal, cum[b][:, h:h + 1] - cum_t[b][h:h + 1, :], -1e30))
             for h in (2 * p, 2 * p + 1)]
        ms.append(jnp.concatenate(m, axis=1).astype(BF16))
        xp = pair(xdt, b, p)
        x2.append(jnp.concatenate([jnp.where(first, xp, 0.0), jnp.where(first, 0.0, xp)],
                                  axis=0).astype(BF16))
    y1 = [jnp.dot(ms[i], x2[i], preferred_element_type=F32) for i in range(len(units))]
    st = [state_s[seq * n_pairs + i] for i in range(len(units))]
    y2 = [_bdot(group(cm, b, p // 2), st[i]) for i, (b, p) in enumerate(units)]
    upd = [_bdot_tn(group(bm, b, p // 2), pair(xdt, b, p) * pair(to_end, b, p)) for b, p in units]
    for i, (b, p) in enumerate(units):
        state_s[seq * n_pairs + i] = st[i] * pair(elast, b, p) + upd[i]
    gw = BRANCH_DIM // M_GROUPS
    for b in range(nb):
        y = jnp.concatenate([y1[b * n_pairs + p] + y2[b * n_pairs + p] * pair(ecum, b, p)
                             for p in range(n_pairs)], axis=1)
        y = (y + xs[b] * dskip_ref[...]) * _silu(z_ref[b])
        y = jnp.concatenate([_rms(y[:, g * gw:(g + 1) * gw], M_NORM_EPS) for g in range(M_GROUPS)],
                            axis=1)
        o_ref[b] = (y * ng_ref[...]).astype(BF16)


def ssd_branch(proj, conv_w, conv_b, head_p, norm_g, batch, seq):
    n = proj.shape[0]
    L = SSD_CHUNK
    expand = jnp.repeat(jnp.eye(M_HEADS, dtype=F32), HEAD_DIM, axis=1)
    dskip = jnp.repeat(head_p[2], HEAD_DIM).reshape(1, BRANCH_DIM)
    proj3 = proj.reshape(batch, seq, PROJ_COLS)

    def rows(width, idx):
        return pl.BlockSpec((batch, L, width), lambda c: (0, c, idx))

    def const(shape):
        return pl.BlockSpec(shape, lambda c: (0,) * len(shape))

    out = pl.pallas_call(
        _ssd_kernel,
        grid=(seq // L,),
        in_specs=[rows(1024, COL_Z // 1024), rows(1024, COL_XS // 1024), rows(1024, COL_BC // 1024),
                  rows(TAIL, COL_TAIL // TAIL),
                  const((M_CONV, 2048)), const((1, 2048)), const((1, M_HEADS)), const((1, M_HEADS)),
                  const((1, BRANCH_DIM)), const((1, BRANCH_DIM)), const((M_HEADS, BRANCH_DIM))],
        out_specs=rows(BRANCH_DIM, 0),
        out_shape=jax.ShapeDtypeStruct((batch, seq, BRANCH_DIM), BF16),
        scratch_shapes=[pltpu.VMEM((batch, L, 2048), F32),
                        pltpu.VMEM((batch * M_HEADS // 2, M_STATE, LANES), F32)],
        compiler_params=_cparams(1),
        name="ssd_branch",
    )(proj3, proj3, proj3, proj3, conv_w, conv_b.reshape(1, -1), head_p[0].reshape(1, -1),
      head_p[1].reshape(1, -1), dskip, norm_g.reshape(1, -1), expand)
    return out.reshape(n, BRANCH_DIM)


def _rwkv_kernel(rkv_ref, tail_ref, mu_rkv_ref, mu_tail_ref, w2_ref, a2_ref, g2_ref, vec_ref,
                 o_ref, prev_rkv_s, prev_tail_s, state_s, *, init=True):
    C = RW_CHUNK
    R = 2 * C
    nb = rkv_ref.shape[0]
    n_pairs = BRANCH_DIM // LANES
    units = [(b, p) for b in range(nb) for p in range(n_pairs)]
    n_u = len(units)

    if init:
        _zero_at_first_step(prev_rkv_s, prev_tail_s, state_s)

    def token_shift(ref, prev_s, mu):
        parts = []
        for b in range(nb):
            raw = ref[b]
            parts.append(raw + (_shift_rows(raw, prev_s[b], 1) - raw) * mu)
            prev_s[b] = raw[C - SUBLANES:]
        return jnp.concatenate(parts, axis=0)

    rkv = token_shift(rkv_ref, prev_rkv_s, mu_rkv_ref[...])
    tail = token_shift(tail_ref, prev_tail_s, mu_tail_ref[...])
    r = rkv[:, :BRANCH_DIM]
    k = rkv[:, BRANCH_DIM:2 * BRANCH_DIM]
    v = rkv[:, 2 * BRANCH_DIM:]

    vec = vec_ref[...]
    w0, a0, k_k, k_a, r_k, ln_w, ln_b = [vec[i:i + 1] for i in range(7)]
    log_w = -_softplus(-(w0 + _bdot(jnp.tanh(tail), w2_ref[...]))) - 0.5
    neg_log_decay = jnp.exp(log_w)
    a = jax.nn.sigmoid(a0 + _bdot(tail, a2_ref[...]))
    g = _bdot(jax.nn.sigmoid(tail), g2_ref[...])

    lane = lax.broadcasted_iota(jnp.int32, (R, R), 1)
    sub = lax.broadcasted_iota(jnp.int32, (R, R), 0)
    same_head = (lane >> 6) == (sub >> 6)
    head_ones2 = jnp.concatenate([same_head.astype(BF16)] * 2, axis=0)
    strict = same_head & (sub > lane)
    incl = same_head & (sub >= lane)
    eye = (sub == lane).astype(F32)
    first = lax.broadcasted_iota(jnp.int32, (C, LANES), 1) < HEAD_DIM

    def head_sum(t):
        return jnp.dot(jnp.concatenate(_split3(t)[:2], axis=1), head_ones2, preferred_element_type=F32)

    rows = nb * C
    ri = lax.broadcasted_iota(jnp.int32, (rows, rows), 0)
    ci = lax.broadcasted_iota(jnp.int32, (rows, rows), 1)
    tri3 = jnp.concatenate([(((ri >> 6) == (ci >> 6)) & (ri >= ci)).astype(BF16)] * 3, axis=1)
    cl = -jnp.dot(tri3, jnp.concatenate(_split3(neg_log_decay), axis=0), preferred_element_type=F32)

    kk = k * k_k
    k2 = k * (1.0 + (a - 1.0) * k_a)

    def unit(t, b, p):
        return t[b * C:(b + 1) * C, p * LANES:(p + 1) * LANES]

    def lanes(t, p):
        return t[:, p * LANES:(p + 1) * LANES]

    def stack(t):
        return jnp.concatenate([jnp.where(first, t, 0.0), jnp.where(first, 0.0, t)], axis=0)

    def dup(t):
        return jnp.concatenate([t, t], axis=0)

    kk_u = [unit(kk, b, p) for b, p in units]
    k2_u = [unit(k2, b, p) for b, p in units]
    r_u = [unit(r, b, p) for b, p in units]
    v_u = [unit(v, b, p) for b, p in units]
    cl_u = [unit(cl, b, p) for b, p in units]
    nld_u = [unit(neg_log_decay, b, p) for b, p in units]
    a_u = [unit(a, b, p) for b, p in units]

    sums = head_sum(jnp.concatenate(
        [t * t for t in kk_u] + [r_u[i] * k2_u[i] * lanes(r_k, units[i][1]) for i in range(n_u)],
        axis=0))
    kk_u = [kk_u[i] * lax.rsqrt(jnp.maximum(sums[i * C:(i + 1) * C], 1e-24)) for i in range(n_u)]
    bonus_u = [sums[(n_u + i) * C:(n_u + i + 1) * C] * v_u[i] for i in range(n_u)]
    b_u = [kk_u[i] * a_u[i] for i in range(n_u)]

    w_inv = [jnp.exp(-t) for t in cl_u]
    lhs_u = [jnp.concatenate([stack(kk_u[i] * jnp.exp(cl_u[i] + nld_u[i])),
                              stack(r_u[i] * jnp.exp(cl_u[i]))], axis=0).astype(BF16)
             for i in range(n_u)]
    rhs_u = [jnp.concatenate([dup(b_u[i] * w_inv[i]), dup(k2_u[i] * w_inv[i])], axis=0).astype(BF16)
             for i in range(n_u)]
    gram_u = [lax.dot_general(lhs_u[i], rhs_u[i], (((1,), (1,)), ((), ())),
                              preferred_element_type=F32) for i in range(n_u)]
    a_b = [jnp.where(strict, t[:R, :R], 0.0) for t in gram_u]
    a_k = [jnp.where(strict, t[:R, R:], 0.0).astype(BF16) for t in gram_u]
    l_bk = [jnp.concatenate([jnp.where(incl, t[R:, :R], 0.0), jnp.where(incl, t[R:, R:], 0.0)],
                            axis=1).astype(BF16) for t in gram_u]
    minv = [eye - t for t in a_b]
    pw = [t.astype(BF16) for t in a_b]
    for _ in range(int(np.log2(C)) - 1):
        pw = [jnp.dot(t, t, preferred_element_type=F32).astype(BF16) for t in pw]
        minv = [minv[i] + jnp.dot(minv[i].astype(BF16), pw[i], preferred_element_type=F32)
                for i in range(n_u)]
    minv = [t.astype(BF16) for t in minv]

    vs = [stack(t).astype(BF16) for t in v_u]
    akv = [jnp.dot(a_k[i], vs[i], preferred_element_type=F32) for i in range(n_u)]
    st = [state_s[i] for i in range(n_u)]
    sp = [_bdot_nt(lhs_u[i], st[i]) for i in range(n_u)]
    u = [jnp.dot(minv[i], (-sp[i][:R] - akv[i]).astype(BF16), preferred_element_type=F32)
         for i in range(n_u)]
    uv = [jnp.concatenate([u[i].astype(BF16), vs[i]], axis=0) for i in range(n_u)]
    ysk = [sp[i][R:] + jnp.dot(l_bk[i], uv[i], preferred_element_type=F32) for i in range(n_u)]
    y_all = jnp.concatenate([t[:C] + t[C:] for t in ysk], axis=0)
    for i in range(n_u):
        cl_last = cl_u[i][C - 1:C]
        w_end = jnp.exp(cl_last - cl_u[i])
        ends = jnp.concatenate([stack(b_u[i] * w_end), stack(k2_u[i] * w_end)], axis=0)
        state_s[i] = st[i] * jnp.exp(cl_last) + _bdot_tn(uv[i], ends)

    dev = y_all - head_sum(y_all) * (1.0 / HEAD_DIM)
    var = head_sum(dev * dev) * (1.0 / HEAD_DIM)
    yn = dev * lax.rsqrt(var + RW_LN_EPS)
    for b in range(nb):
        outs = []
        for p in range(n_pairs):
            i = b * n_pairs + p
            y = yn[i * C:(i + 1) * C] * lanes(ln_w, p) + lanes(ln_b, p)
            outs.append((y + bonus_u[i]) * unit(g, b, p))
        o_ref[b] = jnp.concatenate(outs, axis=1).astype(BF16)


def rwkv_branch(proj, mu_rkv, mu_tail, w2p, a2p, g2p, vec, batch, seq):
    n = proj.shape[0]
    C = RW_CHUNK
    proj3 = proj.reshape(batch, seq, PROJ_COLS)

    def rows(width, idx):
        return pl.BlockSpec((batch, C, width), lambda c: (0, c, idx))

    def const(shape):
        return pl.BlockSpec(shape, lambda c: (0,) * len(shape))

    out = pl.pallas_call(
        _rwkv_kernel,
        grid=(seq // C,),
        in_specs=[rows(3 * BRANCH_DIM, COL_RKV // (3 * BRANCH_DIM)), rows(TAIL, COL_TAIL // TAIL),
                  const((1, 3 * BRANCH_DIM)), const((1, TAIL)),
                  const((TAIL, BRANCH_DIM)), const((TAIL, BRANCH_DIM)), const((TAIL, BRANCH_DIM)),
                  const((7, BRANCH_DIM))],
        out_specs=rows(BRANCH_DIM, 0),
        out_shape=jax.ShapeDtypeStruct((batch, seq, BRANCH_DIM), BF16),
        scratch_shapes=[pltpu.VMEM((batch, SUBLANES, 3 * BRANCH_DIM), F32),
                        pltpu.VMEM((batch, SUBLANES, TAIL), F32),
                        pltpu.VMEM((batch * BRANCH_DIM // LANES, LANES, LANES), F32)],
        compiler_params=_cparams(1),
        name="rwkv_branch",
    )(proj3, proj3, mu_rkv, mu_tail, w2p, a2p, g2p, vec)
    return out.reshape(n, BRANCH_DIM)


def _ret_kernel(q_ref, k_ref, v_ref, g_ref, pos_ref, invf_ref, dmask_ref, qdec_ref, kdec_ref,
                cdec_ref, o_ref, state_s, *, seq=0, init=True):
    L = SSD_CHUNK
    half = RET_DK // 2
    width = RET_HEADS * RET_DK
    nb = q_ref.shape[0]
    units = [(b, h) for b in range(nb) for h in range(RET_HEADS)]
    n_u = len(units)

    if init:
        _zero_at_first_step(state_s)

    lane_w = lax.broadcasted_iota(jnp.int32, (L, width), 1)
    low_half = (lane_w & (RET_DK - 1)) < half
    q, k, qd, kd = [], [], [], []
    for b in range(nb):
        ang = pos_ref[b].astype(F32) * invf_ref[...]
        cos = jnp.concatenate([jnp.cos(ang)] * (width // LANES), axis=1)
        sin = jnp.concatenate([jnp.sin(ang)] * (width // LANES), axis=1)

        def rope(t):
            rot = jnp.where(low_half, -pltpu.roll(t, width - half, 1), pltpu.roll(t, half, 1))
            return t * cos + rot * sin

        q.append(rope(q_ref[b]))
        k.append(rope(k_ref[b]) * (RET_DK ** -0.5))
        qd.append(q[b] * qdec_ref[...])
        kd.append(k[b] * kdec_ref[...])

    lane = lax.broadcasted_iota(jnp.int32, (L, LANES), 1)

    def head(t, b, h, masked=True):
        t = t[b][:, (h // 2) * LANES:(h // 2 + 1) * LANES]
        return jnp.where((lane >> 6) == (h % 2), t, 0.0) if masked else t

    v_u = [v_ref[b, :, h * LANES:(h + 1) * LANES].astype(BF16) for b, h in units]
    scores = [_bdot_nt(head(q, b, h), head(k, b, h, masked=False)) * dmask_ref[h] for b, h in units]
    st = [state_s[seq * RET_HEADS + i] for i in range(n_u)]
    y = [jnp.dot(scores[i].astype(BF16), v_u[i], preferred_element_type=F32) for i in range(n_u)]
    y2 = [_bdot(head(qd, b, h), st[i]) for i, (b, h) in enumerate(units)]
    upd = [_bdot_tn(head(kd, b, h), v_u[i]) for i, (b, h) in enumerate(units)]
    for i, (b, h) in enumerate(units):
        state_s[seq * RET_HEADS + i] = st[i] * cdec_ref[h] + upd[i]
    for b in range(nb):
        outs = [_rms(y[b * RET_HEADS + h] + y2[b * RET_HEADS + h], RET_EPS) for h in range(RET_HEADS)]
        o_ref[b] = (_silu(g_ref[b]) * jnp.concatenate(outs, axis=1)).astype(BF16)


def ret_branch(proj, pos_col, batch, seq):
    n = proj.shape[0]
    L = SSD_CHUNK
    nc = seq // L
    half = RET_DK // 2
    inv_freq = ROPE_BASE ** (-jnp.arange(half, dtype=F32) / half)
    invf = jnp.tile(inv_freq, LANES // half).reshape(1, LANES)
    log_gamma = jnp.log1p(-jnp.exp2(-5.0 - jnp.arange(RET_HEADS, dtype=F32)))
    idx = jnp.arange(L, dtype=F32)
    rel = (idx[:, None] - idx[None, :])[None]
    dmask = jnp.exp(jnp.where(rel >= 0, rel * log_gamma[:, None, None], -jnp.inf))
    qdec = jnp.repeat(jnp.exp((idx[:, None] + 1.0) * log_gamma), RET_DK, axis=1)
    kdec = jnp.repeat(jnp.exp((L - 1.0 - idx)[:, None] * log_gamma), RET_DK, axis=1)
    cdec = jnp.broadcast_to(jnp.exp(L * log_gamma)[:, None, None], (RET_HEADS, 1, LANES))
    proj3 = proj.reshape(batch, seq, PROJ_COLS)

    def rows(width, idx_):
        return pl.BlockSpec((batch, L, width), lambda c: (0, c, idx_))

    def const(shape):
        return pl.BlockSpec(shape, lambda c: (0,) * len(shape))

    out = pl.pallas_call(
        _ret_kernel,
        grid=(seq // L,),
        in_specs=[rows(512, COL_RQ // 512), rows(512, COL_RK // 512), rows(1024, COL_RV // 1024),
                  rows(1024, COL_RG // 1024), rows(1, 0),
                  const((1, LANES)), const((RET_HEADS, L, L)), const((L, 512)), const((L, 512)),
                  const((RET_HEADS, 1, LANES))],
        out_specs=rows(BRANCH_DIM, 0),
        out_shape=jax.ShapeDtypeStruct((batch, seq, BRANCH_DIM), BF16),
        scratch_shapes=[pltpu.VMEM((batch * RET_HEADS, LANES, LANES), F32)],
        compiler_params=_cparams(1),
        name="ret_branch",
    )(proj3, proj3, proj3, proj3, pos_col.reshape(batch, seq, 1), invf, dmask, qdec, kdec, cdec)
    return out.reshape(n, BRANCH_DIM)


N_RW_IN, N_SSD_IN, N_RET_IN = 8, 11, 10


def _branches_kernel(*refs, n_seq):
    a, b, c = N_RW_IN, N_RW_IN + N_SSD_IN, N_RW_IN + N_SSD_IN + N_RET_IN
    rw_in, ssd_in, ret_in = refs[:a], refs[a:b], refs[b:c]
    o_rw, o_ssd, o_ret = refs[c:c + 3]
    rw_s, ssd_s, ret_s = refs[c + 3:c + 6], refs[c + 6:c + 8], refs[c + 8:c + 9]
    _zero_at_first_step(*rw_s, *ssd_s, *ret_s)
    which = pl.program_id(0) % n_seq
    _rwkv_kernel(*rw_in, o_rw, *rw_s, init=False)
    _ssd_kernel(*ssd_in, o_ssd, *ssd_s, seq=which, init=False)
    _ret_kernel(*ret_in, o_ret, *ret_s, seq=which, init=False)


def mixer_branches(proj, pos_col, p, conv_w, conv_b, head_p, norm_g, vec, batch, seq):
    n = proj.shape[0]
    C, L = RW_CHUNK, SSD_CHUNK
    assert batch * C == L, "one SSD / retention chunk of one sequence per RWKV chunk step"
    proj3 = proj.reshape(batch, seq, PROJ_COLS)
    expand = jnp.repeat(jnp.eye(M_HEADS, dtype=F32), HEAD_DIM, axis=1)
    dskip = jnp.repeat(head_p[2], HEAD_DIM).reshape(1, BRANCH_DIM)
    half = RET_DK // 2
    inv_freq = ROPE_BASE ** (-jnp.arange(half, dtype=F32) / half)
    invf = jnp.tile(inv_freq, LANES // half).reshape(1, LANES)
    log_gamma = jnp.log1p(-jnp.exp2(-5.0 - jnp.arange(RET_HEADS, dtype=F32)))
    idx = jnp.arange(L, dtype=F32)
    rel = (idx[:, None] - idx[None, :])[None]
    dmask = jnp.exp(jnp.where(rel >= 0, rel * log_gamma[:, None, None], -jnp.inf))
    qdec = jnp.repeat(jnp.exp((idx[:, None] + 1.0) * log_gamma), RET_DK, axis=1)
    kdec = jnp.repeat(jnp.exp((L - 1.0 - idx)[:, None] * log_gamma), RET_DK, axis=1)
    cdec = jnp.broadcast_to(jnp.exp(L * log_gamma)[:, None, None], (RET_HEADS, 1, LANES))

    def every(width, idx_):
        return pl.BlockSpec((batch, C, width), lambda t: (0, t, idx_))

    def one(width, idx_):
        return pl.BlockSpec((1, L, width), lambda t: (t % batch, t // batch, idx_))

    def const(shape):
        return pl.BlockSpec(shape, lambda t: (0,) * len(shape))

    rw_specs = [every(3 * BRANCH_DIM, COL_RKV // (3 * BRANCH_DIM)), every(TAIL, COL_TAIL // TAIL),
                const((1, 3 * BRANCH_DIM)), const((1, TAIL)),
                const((TAIL, BRANCH_DIM)), const((TAIL, BRANCH_DIM)), const((TAIL, BRANCH_DIM)),
                const((7, BRANCH_DIM))]
    ssd_specs = [one(1024, COL_Z // 1024), one(1024, COL_XS // 1024), one(1024, COL_BC // 1024),
                 one(TAIL, COL_TAIL // TAIL),
                 const((M_CONV, 2048)), const((1, 2048)), const((1, M_HEADS)), const((1, M_HEADS)),
                 const((1, BRANCH_DIM)), const((1, BRANCH_DIM)), const((M_HEADS, BRANCH_DIM))]
    ret_specs = [one(512, COL_RQ // 512), one(512, COL_RK // 512), one(1024, COL_RV // 1024),
                 one(1024, COL_RG // 1024), one(1, 0),
                 const((1, LANES)), const((RET_HEADS, L, L)), const((L, 512)), const((L, 512)),
                 const((RET_HEADS, 1, LANES))]
    assert (len(rw_specs), len(ssd_specs), len(ret_specs)) == (N_RW_IN, N_SSD_IN, N_RET_IN)
    out_sds = jax.ShapeDtypeStruct((batch, seq, BRANCH_DIM), BF16)
    y_rw, y_ssd, y_ret = pl.pallas_call(
        functools.partial(_branches_kernel, n_seq=batch),
        grid=(seq // C,),
        in_specs=rw_specs + ssd_specs + ret_specs,
        out_specs=[every(BRANCH_DIM, 0), one(BRANCH_DIM, 0), one(BRANCH_DIM, 0)],
        out_shape=[out_sds, out_sds, out_sds],
        scratch_shapes=[pltpu.VMEM((batch, SUBLANES, 3 * BRANCH_DIM), F32),
                        pltpu.VMEM((batch, SUBLANES, TAIL), F32),
                        pltpu.VMEM((batch * BRANCH_DIM // LANES, LANES, LANES), F32),
                        pltpu.VMEM((batch, L, 2048), F32),
                        pltpu.VMEM((batch * M_HEADS // 2, M_STATE, LANES), F32),
                        pltpu.VMEM((batch * RET_HEADS, LANES, LANES), F32)],
        compiler_params=_cparams(1),
        name="mixer_branches",
    )(proj3, proj3, p["mu_rkv"], p["mu_tail"], p["w2p"], p["a2p"], p["g2p"], vec,
      proj3, proj3, proj3, proj3, conv_w, conv_b.reshape(1, -1), head_p[0].reshape(1, -1),
      head_p[1].reshape(1, -1), dskip, norm_g.reshape(1, -1), expand,
      proj3, proj3, proj3, proj3, pos_col.reshape(batch, seq, 1), invf, dmask, qdec, kdec, cdec)
    return [t.reshape(n, BRANCH_DIM) for t in (y_ssd, y_rw, y_ret)]


def _merge_kernel(h_ref, y0_ref, y1_ref, y2_ref, wg_ref, wb_ref, o_ref):
    h = h_ref[...]
    acc = None
    for b, y_ref in enumerate((y0_ref, y1_ref, y2_ref)):
        gate = jax.nn.sigmoid(jnp.dot(h, wg_ref[b], preferred_element_type=F32))
        term = gate * jnp.dot(y_ref[...], wb_ref[b], preferred_element_type=F32)
        acc = term if acc is None else acc + term
    o_ref[...] = acc.astype(BF16)


def merge_branches(h, ys, wg, wb, seq):
    n, d = h.shape
    tm = min(1024, seq)
    tn = 512
    yspec = pl.BlockSpec((tm, BRANCH_DIM), lambda i, j: (i, 0))
    return pl.pallas_call(
        _merge_kernel,
        grid=(n // tm, d // tn),
        in_specs=[pl.BlockSpec((tm, d), lambda i, j: (i, 0)), yspec, yspec, yspec,
                  pl.BlockSpec((3, d, tn), lambda i, j: (0, 0, j)),
                  pl.BlockSpec((3, BRANCH_DIM, tn), lambda i, j: (0, 0, j))],
        out_specs=pl.BlockSpec((tm, tn), lambda i, j: (i, j)),
        out_shape=jax.ShapeDtypeStruct((n, d), BF16),
        compiler_params=_cparams(2),
        name="merge_branches",
    )(h, *ys, wg, wb)


def _proj_residual_kernel(a_ref, w_ref, x_ref, g_ref, gt_ref, o_ref):
    y = jnp.dot(a_ref[...], w_ref[...], preferred_element_type=F32)
    o_ref[...] = x_ref[...] + gt_ref[0] * (_rms(y, NORM_EPS) * g_ref[...])


def proj_residual(a, w, x, g, gt, seq):
    n, d = x.shape
    tm = min(512, seq)
    per_batch = seq // tm
    return pl.pallas_call(
        _proj_residual_kernel,
        grid=(n // tm,),
        in_specs=[pl.BlockSpec((tm, d), lambda i: (i, 0)),
                  pl.BlockSpec((d, d), lambda i: (0, 0)),
                  pl.BlockSpec((tm, d), lambda i: (i, 0)),
                  pl.BlockSpec((1, d), lambda i: (0, 0)),
                  pl.BlockSpec((1, 1, d), lambda i: (i // per_batch, 0, 0))],
        out_specs=pl.BlockSpec((tm, d), lambda i: (i, 0)),
        out_shape=jax.ShapeDtypeStruct((n, d), F32),
        compiler_params=_cparams(1),
        name="proj_residual",
    )(a, w, x, g, gt)


def _gelu_tanh(t):
    return 0.5 * t * (1.0 + jnp.tanh(np.sqrt(2.0 / np.pi).astype(np.float32)
                                     * (t + 0.044715 * (t * t * t))))


def _ffn_kernel(x_ref, g2_ref, sc_ref, sh_ref, wg_ref, wu_ref, cw_ref, cb_ref, wd_ref, g3_ref,
                gt_ref, o_ref, h_s, acc_s, carry_s, pre_even_s, pre_odd_s, act_s, *, per_batch, nj):
    i = pl.program_id(0)
    s = pl.program_id(1)
    tm = x_ref.shape[0]
    seq_start = (i % per_batch) == 0

    n_chunks = 2
    n_pieces = 4
    rc = tm // n_chunks

    tf = wg_ref.shape[1]

    def up_proj_piece(pre_s, c):
        half, cols = c // 2, slice((c % 2) * (tf // 2), (c % 2 + 1) * (tf // 2))
        w_ref = wu_ref if half else wg_ref
        pre_s[half, :, cols] = jnp.dot(h_s[...], w_ref[:, cols], preferred_element_type=F32)

    def up_proj(pre_s):
        for c in range(n_pieces):
            up_proj_piece(pre_s, c)

    def conv(pre_s, half, c):
        off = pl.multiple_of((s - 1) * SUBLANES, SUBLANES)
        pre = pre_s[half, c * rc:(c + 1) * rc, :]
        if c == 0:
            prev = jnp.where(seq_start, 0.0, carry_s[half, pl.ds(off, SUBLANES), :])
        else:
            prev = pre_s[half, c * rc - SUBLANES:c * rc, :]
        if c == n_chunks - 1:
            carry_s[half, pl.ds(off, SUBLANES), :] = pre[rc - SUBLANES:]
        cw = cw_ref[half]
        return (cw[0:1] * _shift_rows(pre, prev, 2) + cw[1:2] * _shift_rows(pre, prev, 1)
                + cw[2:3] * pre + cb_ref[half])

    def down_proj(pre_s, pre_next_s=None):
        for c in range(n_chunks):
            rows = slice(c * rc, (c + 1) * rc)
            act_s[rows, :] = (_gelu_tanh(conv(pre_s, 0, c)) * conv(pre_s, 1, c)).astype(BF16)
            if pre_next_s is not None:
                for piece in range(c * n_pieces // n_chunks, (c + 1) * n_pieces // n_chunks):
                    up_proj_piece(pre_next_s, piece)
            acc_s[rows, :] += jnp.dot(act_s[rows, :], wd_ref[...], preferred_element_type=F32)

    @pl.when(s == 0)
    def _():
        h_s[...] = _modulated_norm(x_ref[...], g2_ref[...], sc_ref[0], sh_ref[0]).astype(BF16)
        acc_s[...] = jnp.zeros_like(acc_s)
        up_proj(pre_even_s)

    @pl.when((s > 0) & (s < nj) & (s % 2 == 1))
    def _():
        down_proj(pre_even_s, pre_odd_s)

    @pl.when((s > 0) & (s < nj) & (s % 2 == 0))
    def _():
        down_proj(pre_odd_s, pre_even_s)

    @pl.when(s == nj)
    def _():
        down_proj(pre_even_s if (nj - 1) % 2 == 0 else pre_odd_s)
        o_ref[...] = x_ref[...] + gt_ref[0] * (_rms(acc_s[...], NORM_EPS) * g3_ref[...])


def conv_geglu_residual(x, g2, sc, sh, wg, wu, cw, cb, wd, g3, gt, seq):
    n, d = x.shape
    fp = wg.shape[1]
    tm = min(512, seq)
    tf = FFN_TILE
    nj = fp // tf
    per_batch = seq // tm
    mod = lambda i, s: (i // per_batch, 0, 0)
    ahead = lambda i, s: (0, jnp.minimum(s, nj - 1))
    behind = lambda i, s: (0, 0, jnp.maximum(s - 1, 0))
    return pl.pallas_call(
        functools.partial(_ffn_kernel, per_batch=per_batch, nj=nj),
        grid=(n // tm, nj + 1),
        in_specs=[pl.BlockSpec((tm, d), lambda i, s: (i, 0)),
                  pl.BlockSpec((1, d), lambda i, s: (0, 0)),
                  pl.BlockSpec((1, 1, d), mod), pl.BlockSpec((1, 1, d), mod),
                  pl.BlockSpec((d, tf), ahead),
                  pl.BlockSpec((d, tf), ahead),
                  pl.BlockSpec((2, 3, tf), behind),
                  pl.BlockSpec((2, 1, tf), behind),
                  pl.BlockSpec((tf, d), lambda i, s: (jnp.maximum(s - 1, 0), 0)),
                  pl.BlockSpec((1, d), lambda i, s: (0, 0)),
                  pl.BlockSpec((1, 1, d), mod)],
        out_specs=pl.BlockSpec((tm, d), lambda i, s: (i, 0)),
        out_shape=jax.ShapeDtypeStruct((n, d), F32),
        scratch_shapes=[pltpu.VMEM((tm, d), BF16), pltpu.VMEM((tm, d), F32),
                        pltpu.VMEM((2, nj * SUBLANES, tf), F32),
                        pltpu.VMEM((2, tm, tf), F32), pltpu.VMEM((2, tm, tf), F32),
                        pltpu.VMEM((tm, tf), BF16)],
        compiler_params=_cparams(2),
        name="conv_geglu",
    )(x, g2, sc, sh, wg, wu, cw, cb, wd, g3, gt)


def _layer_params(l, w_in, rwkv_mu, rwkv_w2, rwkv_a2, rwkv_g2, w_gate, w_branch, w_out, w_up,
                  f_conv_w, f_conv_b, w_down):
    m_cols = BRANCH_DIM + 2048 + M_HEADS
    rw_cols = 3 * BRANCH_DIM + LORA_W + LORA_A + LORA_G
    wm, wr, wt = w_in[l, :, :m_cols], w_in[l, :, m_cols:m_cols + rw_cols], w_in[l, :, m_cols + rw_cols:]
    n_lora = LORA_W + LORA_A + LORA_G
    w_cat = jnp.concatenate([
        wm[:, :BRANCH_DIM + 2048], wr[:, :3 * BRANCH_DIM], wt,
        wr[:, 3 * BRANCH_DIM:], wm[:, BRANCH_DIM + 2048:],
        jnp.zeros((D_MODEL, TAIL - n_lora - M_HEADS), F32)], axis=1).astype(BF16)
    mu = rwkv_mu[l]
    mu_rkv = mu[:3 * BRANCH_DIM].reshape(1, -1)
    mu_tail = jnp.concatenate([mu[3 * BRANCH_DIM:], jnp.zeros((TAIL - n_lora,), F32)]).reshape(1, -1)

    def pad_rows(w, start):
        return jnp.zeros((TAIL, BRANCH_DIM), F32).at[start:start + w.shape[0]].set(w).astype(BF16)

    fpad = FFN_PAD - FFN_DIM
    wg = jnp.pad(w_up[l, :, :FFN_DIM], ((0, 0), (0, fpad))).astype(BF16)
    wu = jnp.pad(w_up[l, :, FFN_DIM:], ((0, 0), (0, fpad))).astype(BF16)
    cw = jnp.pad(jnp.stack([f_conv_w[l, :, :FFN_DIM], f_conv_w[l, :, FFN_DIM:]]),
                 ((0, 0), (0, 0), (0, fpad)))
    cb = jnp.pad(jnp.stack([f_conv_b[l, :FFN_DIM], f_conv_b[l, FFN_DIM:]]),
                 ((0, 0), (0, fpad)))[:, None, :]
    wd = jnp.pad(w_down[l], ((0, fpad), (0, 0))).astype(BF16)
    return dict(w_cat=w_cat, mu_rkv=mu_rkv, mu_tail=mu_tail,
                w2p=pad_rows(rwkv_w2[l], TAIL_WD), a2p=pad_rows(rwkv_a2[l], TAIL_AD),
                g2p=pad_rows(rwkv_g2[l], TAIL_GD),
                wgate=w_gate[l].astype(BF16), wbranch=w_branch[l].astype(BF16),
                wout=w_out[l].astype(BF16), wg=wg, wu=wu, cw=cw, cb=cb, wd=wd)


def kernel(x, c, positions, w_ada, b_ada, norm_g, w_in, m_conv_w, m_conv_b, m_head, m_norm_g,
           rwkv_mu, rwkv_w2, rwkv_a2, rwkv_g2, rwkv_vec, w_gate, w_branch, w_out, w_up, f_conv_w,
           f_conv_b, w_down):
    batch, seq, d = x.shape
    depth = w_ada.shape[0]
    n = batch * seq
    xf = x.reshape(n, d)
    pos_col = positions.reshape(n, 1)
    c_pad = jnp.zeros((SUBLANES, d), F32).at[:batch].set(c)
    for l in range(depth):
        p = _layer_params(l, w_in, rwkv_mu, rwkv_w2, rwkv_a2, rwkv_g2, w_gate, w_branch, w_out,
                          w_up, f_conv_w, f_conv_b, w_down)
        mod = ada_mod(c_pad, w_ada, b_ada, l)[:batch].reshape(batch, 6, 1, d)
        sh_m, sc_m, gt_m, sh_f, sc_f, gt_f = [mod[:, i] for i in range(6)]
        g = norm_g[l].reshape(4, 1, d)
        proj, h = norm_proj(xf, g[0], sc_m, sh_m, p["w_cat"], seq)
        y_ssd, y_rwkv, y_ret = mixer_branches(proj, pos_col, p, m_conv_w[l], m_conv_b[l], m_head[l],
                                              m_norm_g[l], rwkv_vec[l], batch, seq)
        merged = merge_branches(h, (y_ssd, y_rwkv, y_ret), p["wgate"], p["wbranch"], seq)
        xf = proj_residual(merged, p["wout"], xf, g[1], gt_m, seq)
        xf = conv_geglu_residual(xf, g[2], sc_f, sh_f, p["wg"], p["wu"], p["cw"], p["cb"], p["wd"],
                                 g[3], gt_f, seq)
    return xf.reshape(batch, seq, d)
```
